```python
import math
import jax, jax.numpy as jnp
from jax import lax
import numpy as np

D_MODEL = 1024
BATCH = 4
SEQ = 4096
DEPTH = 1
DEC_BATCH = 8
DEC_SEQ = 8192
PAST_LEN = 128

HEAD_DIM = 64
H_A = 8
N_GROUPS_B = 3
HG_B = 4
H_B = N_GROUPS_B * HG_B
WINDOWS_B = (128, 512, 2048)
DILATIONS_B = (1, 4, 16)
BLK = 64
GRID_W = 64
WIN_ROWS = 8
WIN_COLS = 16
NUM_BUCKETS = 32
T5_MAX_DIST = 1024
D_FF = 4 * D_MODEL
EPS = 1e-6
NEG = -1e30
QA_W = H_A * HEAD_DIM
QB_W = H_B * HEAD_DIM
IN_W = 3 * QA_W + 3 * QB_W + 2 * D_MODEL

kernel_name = "hybrid_natten_dilated_encoder"


def rmsnorm(x, g):
    xf = x.astype(jnp.float32)
    y = xf * lax.rsqrt(jnp.mean(xf * xf, axis=-1, keepdims=True) + EPS)
    return (y * g.astype(jnp.float32)).astype(x.dtype)


def t5_buckets(rel):
    half = NUM_BUCKETS // 2
    ret = np.where(rel > 0, half, 0)
    n = np.abs(rel)
    max_exact = half // 2
    large = max_exact + (np.log(np.maximum(n, 1) / max_exact)
                         / np.log(T5_MAX_DIST / max_exact) * (half - max_exact)).astype(np.int32)
    large = np.minimum(large, half - 1)
    return (ret + np.where(n < max_exact, n, large)).astype(np.int32)


def neighborhood_attention(q, k, v, rpb):
    B, H, T, hd = q.shape
    rows = T // GRID_W
    kr = min(WIN_ROWS, rows)
    qg = q.reshape(B, H, rows, GRID_W, hd)
    kg = k.reshape(B, H, rows, GRID_W, hd)
    vg = v.reshape(B, H, rows, GRID_W, hd)
    col = np.arange(GRID_W)
    col_start = np.clip(col - WIN_COLS // 2, 0, GRID_W - WIN_COLS)
    col_idx = col_start[:, None] + np.arange(WIN_COLS)[None, :]
    col_off = col_idx - col[:, None]
    bias_c = rpb[:, :, col_off + WIN_COLS - 1]
    scale = hd ** -0.5

    def one_row(r):
        start = jnp.clip(r - kr // 2, 0, rows - kr)
        k_rows = lax.dynamic_slice_in_dim(kg, start, kr, axis=2)
        v_rows = lax.dynamic_slice_in_dim(vg, start, kr, axis=2)
        k_win = k_rows[:, :, :, col_idx, :]
        v_win = v_rows[:, :, :, col_idx, :]
        q_row = lax.dynamic_index_in_dim(qg, r, axis=2, keepdims=False)
        s = jnp.einsum('bhcd,bhrckd->bhcrk', q_row, k_win).astype(jnp.float32) * scale
        row_off = start + jnp.arange(kr) - r + WIN_ROWS - 1
        b = jnp.take(bias_c, row_off, axis=1).transpose(0, 2, 1, 3)
        s = s + b[None].astype(jnp.float32)
        p = jax.nn.softmax(s.reshape(B, H, GRID_W, kr * WIN_COLS), axis=-1)
        p = p.reshape(B, H, GRID_W, kr, WIN_COLS).astype(v.dtype)
        return jnp.einsum('bhcrk,bhrckd->bhcd', p, v_win)

    out = lax.map(one_row, jnp.arange(rows))
    return out.transpose(1, 2, 0, 3, 4).reshape(B, H, T, hd)


def dilated_group_attention(q, k, v, bias_tab, dil, half_keys):
    B, H, T, hd = q.shape
    L = T // dil
    nb = -(-L // BLK)
    Lp = nb * BLK

    def to_sub(x):
        x = x.reshape(B, H, L, dil, hd).transpose(0, 1, 3, 2, 4)
        return jnp.pad(x, ((0, 0), (0, 0), (0, 0), (0, Lp - L), (0, 0)))

    def key_blocks(x):
        xp = jnp.pad(to_sub(x), ((0, 0), (0, 0), (0, 0), (BLK, BLK), (0, 0)))
        xp = xp.reshape(B, H, dil, nb + 2, BLK, hd)
        return jnp.concatenate([xp[:, :, :, :-2], xp[:, :, :, 1:-1], xp[:, :, :, 2:]], axis=4)

    qs = to_sub(q).reshape(B, H, dil, nb, BLK, hd)
    kb = key_blocks(k)
    vb = key_blocks(v)
    a = np.arange(BLK)[:, None]
    bidx = np.arange(3 * BLK)[None, :]
    rel = bidx - BLK - a
    key_pos = np.arange(nb)[:, None, None] * BLK + bidx[None] - BLK
    valid = (np.abs(rel) <= half_keys)[None] & (key_pos >= 0) & (key_pos < L)
    bias = bias_tab[t5_buckets(rel * dil)].transpose(2, 0, 1)
    s = jnp.einsum('bhrnqd,bhrnkd->bhrnqk', qs, kb).astype(jnp.float32) * (hd ** -0.5)
    s = s + bias[None, :, None, None].astype(jnp.float32)
    s = jnp.where(valid, s, NEG)
    m = jnp.max(s, axis=-1, keepdims=True)
    e = jnp.exp(s - m)
    den = jnp.sum(e, axis=-1, keepdims=True)
    o = jnp.einsum('bhrnqk,bhrnkd->bhrnqd', (e / den).astype(v.dtype), vb)

    def back(t):
        c = t.shape[-1]
        t = t.reshape(B, H, dil, Lp, c)[:, :, :, :L]
        return t.transpose(0, 1, 3, 2, 4).reshape(B, H, T, c)

    return back(o), back(m), back(den)


def encoder_layer(x, norm_mix, w_in, q_norm_a, k_norm_a, q_norm_b, k_norm_b, rpb_a, t5_bias,
                  w_branch_a, w_branch_b, w_out, norm_mlp, w_up, w_down):
    B, T, _ = x.shape
    h = rmsnorm(x, norm_mix)
    proj = h @ w_in
    cuts = list(np.cumsum([QA_W, QA_W, QA_W, QB_W, QB_W, QB_W, D_MODEL]))
    qa, ka, va, qb, kb, vb, ga, gb = jnp.split(proj, cuts, axis=-1)

    def heads(t, n):
        return t.reshape(B, T, n, HEAD_DIM).transpose(0, 2, 1, 3)

    qa = rmsnorm(heads(qa, H_A), q_norm_a)
    ka = rmsnorm(heads(ka, H_A), k_norm_a)
    oa = neighborhood_attention(qa, ka, heads(va, H_A), rpb_a)
    oa = oa.transpose(0, 2, 1, 3).reshape(B, T, QA_W)

    qb = rmsnorm(heads(qb, H_B), q_norm_b)
    kb = rmsnorm(heads(kb, H_B), k_norm_b)
    vb = heads(vb, H_B)
    outs, maxs, dens = [], [], []
    for g in range(N_GROUPS_B):
        sl = slice(g * HG_B, (g + 1) * HG_B)
        o, m, den = dilated_group_attention(qb[:, sl], kb[:, sl], vb[:, sl], t5_bias[:, sl],
                                            DILATIONS_B[g], (WINDOWS_B[g] // 2) // DILATIONS_B[g])
        outs.append(o.astype(jnp.float32)); maxs.append(m); dens.append(den)
    m_all = jnp.max(jnp.stack(maxs, 0), axis=0)
    wts = [d * jnp.exp(m - m_all) for d, m in zip(dens, maxs)]
    ob = sum(w * o for w, o in zip(wts, outs)) / sum(wts)
    ob = ob.astype(x.dtype).transpose(0, 2, 1, 3).reshape(B, T, HG_B * HEAD_DIM)

    merged = jax.nn.sigmoid(ga) * (oa @ w_branch_a) + jax.nn.sigmoid(gb) * (ob @ w_branch_b)
    x = x + merged @ w_out

    hm = rmsnorm(x, norm_mlp)
    u = jax.nn.relu(hm @ w_up)
    return x + (u * u) @ w_down


def setup_inputs(seed: int = 0) -> dict:
    key = jax.random.key(seed)
    ks = jax.random.split(key, 16)
    f32 = jnp.float32

    def nrm(k, shape, scale):
        return jax.random.normal(k, shape, f32) * scale

    return {
        "x_prompt": nrm(ks[0], (BATCH, SEQ, D_MODEL), 1.0),
        "x_sample": nrm(ks[1], (DEC_BATCH, DEC_SEQ, D_MODEL), 1.0),
        "norm_mix": 1.0 + nrm(ks[2], (DEPTH, D_MODEL), 0.02),
        "w_in": nrm(ks[3], (DEPTH, D_MODEL, IN_W), D_MODEL ** -0.5),
        "q_norm_a": 1.0 + nrm(ks[4], (DEPTH, HEAD_DIM), 0.02),
        "k_norm_a": 1.0 + nrm(ks[5], (DEPTH, HEAD_DIM), 0.02),
        "q_norm_b": 1.0 + nrm(ks[6], (DEPTH, HEAD_DIM), 0.02),
        "k_norm_b": 1.0 + nrm(ks[7], (DEPTH, HEAD_DIM), 0.02),
        "rpb_a": nrm(ks[8], (DEPTH, H_A, 2 * WIN_ROWS - 1, 2 * WIN_COLS - 1), 0.1),
        "t5_bias": nrm(ks[9], (NUM_BUCKETS, H_B), 0.1),
        "w_branch_a": nrm(ks[10], (DEPTH, QA_W, D_MODEL), QA_W ** -0.5),
        "w_branch_b": nrm(ks[11], (DEPTH, HG_B * HEAD_DIM, D_MODEL), (HG_B * HEAD_DIM) ** -0.5),
        "w_out": nrm(ks[12], (DEPTH, D_MODEL, D_MODEL), D_MODEL ** -0.5),
        "norm_mlp": 1.0 + nrm(ks[13], (DEPTH, D_MODEL), 0.02),
        "w_up": nrm(ks[14], (DEPTH, D_MODEL, D_FF), D_MODEL ** -0.5),
        "w_down": nrm(ks[15], (DEPTH, D_FF, D_MODEL), D_FF ** -0.5),
    }


def reference(x_prompt, x_sample, norm_mix, w_in, q_norm_a, k_norm_a, q_norm_b, k_norm_b, rpb_a,
              t5_bias, w_branch_a, w_branch_b, w_out, norm_mlp, w_up, w_down):
    y_prompt = x_prompt
    y_sample = x_sample
    for l in range(DEPTH):
        params = (norm_mix[l], w_in[l], q_norm_a[l], k_norm_a[l], q_norm_b[l], k_norm_b[l],
                  rpb_a[l], t5_bias, w_branch_a[l], w_branch_b[l], w_out[l], norm_mlp[l],
                  w_up[l], w_down[l])
        y_prompt = encoder_layer(y_prompt, *params)
        y_sample = encoder_layer(y_sample, *params)
    return (y_prompt, y_sample)
```

```python
import functools
import math

import numpy as np
import jax
import jax.numpy as jnp
from jax import lax
from jax.experimental import pallas as pl
from jax.experimental.pallas import tpu as pltpu

D_MODEL = 1024
HEAD_DIM = 64
H_A = 8
N_GROUPS_B = 3
HG_B = 4
H_B = N_GROUPS_B * HG_B
WINDOWS_B = (128, 512, 2048)
DILATIONS_B = (1, 4, 16)
GRID_W = 64
WIN_ROWS = 8
WIN_COLS = 16
NUM_BUCKETS = 32
T5_MAX_DIST = 1024
D_FF = 4 * D_MODEL
EPS = 1e-6
NEG = -1e30
QA_W = H_A * HEAD_DIM
QB_W = H_B * HEAD_DIM
GB_W = HG_B * HEAD_DIM
IN_W = 3 * QA_W + 3 * QB_W + 2 * D_MODEL
HALF_KEYS = 64
assert all((w // 2) // d == HALF_KEYS for w, d in zip(WINDOWS_B, DILATIONS_B))

LANES = 128
NORM_CHUNK = 256
TM_PROJ = 512
TM_MLP = 512
FF_CHUNK = 1024
QROWS_A = 4
KROWS_A = QROWS_A + 8
QBLK_B = 128
KBLK_B = QBLK_B + 2 * HALF_KEYS
CHUNK_B = 2048
VMEM_LIMIT = 56 * 1024 * 1024

_F32 = jnp.float32
_BF16 = jnp.bfloat16


def _resident(shape):
    nd = len(shape)
    return pl.BlockSpec(shape, lambda *_: (0,) * nd, pipeline_mode=pl.Buffered(1))


def _proj_kernel(x_ref, gmix_ref, w_ref, gqa_ref, gka_ref, gqb_ref, gkb_ref, bd_ref,
                 qa_ref, ka_ref, va_ref, b0_ref, b1_ref, b2_ref, gate_ref, scr_ref):
    tm = x_ref.shape[1]
    x = x_ref[0]
    ms = jnp.mean(x * x, axis=-1, keepdims=True)
    h = (x * lax.rsqrt(ms + EPS) * gmix_ref[...]).astype(_BF16)

    def proj(c0, width):
        return jnp.dot(h, w_ref[:, c0:c0 + width], preferred_element_type=_F32)

    def head_norm(t, gain_ref, scale):
        msq = jnp.dot((t * t).astype(_BF16), bd_ref[...], preferred_element_type=_F32)
        return t * lax.rsqrt(msq + EPS) * (gain_ref[...] * scale)

    qk_scale = HEAD_DIM ** -0.5

    t = proj(0, QA_W)
    for c in range(QA_W // NORM_CHUNK):
        sl = slice(c * NORM_CHUNK, (c + 1) * NORM_CHUNK)
        qa_ref[0, :, sl] = head_norm(t[:, sl], gqa_ref, qk_scale).astype(_BF16)
    t = proj(QA_W, QA_W)
    for c in range(QA_W // NORM_CHUNK):
        sl = slice(c * NORM_CHUNK, (c + 1) * NORM_CHUNK)
        ka_ref[0, :, sl] = head_norm(t[:, sl], gka_ref, 1.0).astype(_BF16)
    va_ref[0] = proj(2 * QA_W, QA_W).astype(_BF16)

    outs = (b0_ref, b1_ref, b2_ref)
    base = 3 * QA_W
    slab = 0
    for kind in range(3):
        t = proj(base + kind * QB_W, QB_W)
        for g, dil in enumerate(DILATIONS_B):
            tg = t[:, g * GB_W:(g + 1) * GB_W]
            if kind == 0:
                tg = head_norm(tg, gqb_ref, qk_scale)
            elif kind == 1:
                tg = head_norm(tg, gkb_ref, 1.0)
            lane0 = kind * GB_W
            if dil == 1:
                outs[g][0, 0, :, lane0:lane0 + GB_W] = tg.astype(_BF16)
                continue
            for half in range(GB_W // LANES):
                scr_ref[slab] = tg[:, half * LANES:(half + 1) * LANES]
                for r in range(dil):
                    sub = scr_ref[slab, pl.ds(r, tm // dil, stride=dil), :]
                    l0 = lane0 + half * LANES
                    outs[g][0, r, :, l0:l0 + LANES] = sub.astype(_BF16)
                slab += 1

    base = 3 * QA_W + 3 * QB_W
    gate_ref[0] = jax.nn.sigmoid(proj(base, 2 * D_MODEL)).astype(_BF16)


def _proj_call(x, gmix, w_in, gqa, gka, gqb, gkb, bd):
    B, T, D = x.shape
    tm = TM_PROJ
    assert T % tm == 0 and tm % (16 * max(DILATIONS_B)) == 0
    n_slabs = 3 * sum(1 for d in DILATIONS_B if d > 1) * (GB_W // LANES)
    tok = lambda w: pl.BlockSpec((1, tm, w), lambda b, i: (b, i, 0))
    sub = lambda d: pl.BlockSpec((1, d, tm // d, 3 * GB_W), lambda b, i: (b, 0, i, 0))
    out_shape = (
        jax.ShapeDtypeStruct((B, T, QA_W), _BF16),
        jax.ShapeDtypeStruct((B, T, QA_W), _BF16),
        jax.ShapeDtypeStruct((B, T, QA_W), _BF16),
    ) + tuple(jax.ShapeDtypeStruct((B, d, T // d, 3 * GB_W), _BF16) for d in DILATIONS_B) + (
        jax.ShapeDtypeStruct((B, T, 2 * D_MODEL), _BF16),
    )
    return pl.pallas_call(
        _proj_kernel,
        grid=(B, T // tm),
        in_specs=[tok(D), _resident((1, D)), _resident((D, IN_W)),
                  _resident((1, NORM_CHUNK)), _resident((1, NORM_CHUNK)),
                  _resident((1, NORM_CHUNK)), _resident((1, NORM_CHUNK)),
                  _resident((NORM_CHUNK, NORM_CHUNK))],
        out_specs=(tok(QA_W), tok(QA_W), tok(QA_W)) + tuple(sub(d) for d in DILATIONS_B)
        + (tok(2 * D_MODEL),),
        out_shape=out_shape,
        scratch_shapes=[pltpu.VMEM((n_slabs, tm, LANES), _F32)],
        compiler_params=pltpu.CompilerParams(
            dimension_semantics=("arbitrary", "arbitrary"), vmem_limit_bytes=VMEM_LIMIT),
        name="proj",
    )(x, gmix, w_in, gqa, gka, gqb, gkb, bd)


def _pair_attention(q, kw, vw, bias_of_head):
    m_rows = q.shape[0]
    lane = lax.broadcasted_iota(jnp.int32, (m_rows, LANES), 1)
    first = lane < HEAD_DIM
    o_h, m_h, l_h = [], [], []
    for hh in range(2):
        qh = jnp.where(first if hh == 0 else jnp.logical_not(first), q, jnp.zeros_like(q))
        s = lax.dot_general(qh, kw, (((1,), (1,)), ((), ())), preferred_element_type=_F32)
        s = s + bias_of_head(hh)
        m = jnp.max(s, axis=-1, keepdims=True)
        p = jnp.exp(s - m)
        l = jnp.sum(p, axis=-1, keepdims=True)
        o_h.append(jnp.dot(p.astype(_BF16), vw, preferred_element_type=_F32))
        m_h.append(jnp.broadcast_to(m, (m_rows, LANES)))
        l_h.append(jnp.broadcast_to(l, (m_rows, LANES)))
    pick = lambda pair: jnp.where(first, pair[0], pair[1])
    return pick(o_h), pick(m_h), pick(l_h)


def _natten_kernel(q_ref, k_ref, v_ref, bias_ref, o_ref, *, rows):
    i = pl.program_id(2)
    krow0 = jnp.clip(QROWS_A * i - WIN_ROWS // 2, 0, rows - KROWS_A)
    k0 = pl.multiple_of(krow0 * GRID_W, GRID_W)
    nk = KROWS_A * GRID_W
    kw = k_ref[0, pl.ds(k0, nk), :]
    vw = v_ref[0, pl.ds(k0, nk), :]
    o, _, l = _pair_attention(q_ref[0], kw, vw, lambda hh: bias_ref[0, hh])
    o_ref[0] = (o / l).astype(_BF16)


def _natten_bias(rpb):
    fake_rows = 3 * KROWS_A
    nblk = fake_rows // QROWS_A
    tiles_idx, tiles_ok = [], []
    for blk in (0, nblk // 2, nblk - 1):
        krow0 = int(np.clip(QROWS_A * blk - WIN_ROWS // 2, 0, fake_rows - KROWS_A))
        r = (QROWS_A * blk + np.arange(QROWS_A))[:, None, None, None]
        c = np.arange(GRID_W)[None, :, None, None]
        kr = (krow0 + np.arange(KROWS_A))[None, None, :, None]
        kc = np.arange(GRID_W)[None, None, None, :]
        start = np.clip(r - WIN_ROWS // 2, 0, fake_rows - WIN_ROWS)
        col_start = np.clip(c - WIN_COLS // 2, 0, GRID_W - WIN_COLS)
        ok = (kr >= start) & (kr < start + WIN_ROWS) & (kc >= col_start) & (kc < col_start + WIN_COLS)
        row_off = np.clip(kr - r + WIN_ROWS - 1, 0, 2 * WIN_ROWS - 2)
        col_off = np.clip(kc - c + WIN_COLS - 1, 0, 2 * WIN_COLS - 2)
        idx = np.broadcast_to(row_off * (2 * WIN_COLS - 1) + col_off, ok.shape)
        shape2 = (QROWS_A * GRID_W, KROWS_A * GRID_W)
        tiles_idx.append(idx.reshape(shape2))
        tiles_ok.append(ok.reshape(shape2))
    idx = np.stack(tiles_idx).astype(np.int32)
    ok = np.stack(tiles_ok)
    flat = rpb.reshape(H_A, -1).astype(_F32)
    vals = jnp.take(flat, jnp.asarray(idx.reshape(-1)), axis=1).reshape((H_A,) + idx.shape)
    vals = jnp.where(jnp.asarray(ok)[None], vals, NEG)
    return jnp.transpose(vals, (1, 0, 2, 3))


def _natten_call(qa, ka, va, bias):
    B, T, _ = qa.shape
    rows = T // GRID_W
    assert T % GRID_W == 0 and rows % QROWS_A == 0 and rows >= 3 * KROWS_A // 2
    nblk = rows // QROWS_A
    mq, nk = QROWS_A * GRID_W, KROWS_A * GRID_W
    variant = lambda i: jnp.where(i == 0, 0, jnp.where(i == nblk - 1, 2, 1))
    return pl.pallas_call(
        functools.partial(_natten_kernel, rows=rows),
        grid=(B, H_A // 2, nblk),
        in_specs=[pl.BlockSpec((1, mq, LANES), lambda b, p, i: (b, i, p)),
                  pl.BlockSpec((1, T, LANES), lambda b, p, i: (b, 0, p)),
                  pl.BlockSpec((1, T, LANES), lambda b, p, i: (b, 0, p)),
                  pl.BlockSpec((1, 2, mq, nk), lambda b, p, i: (variant(i), p, 0, 0))],
        out_specs=pl.BlockSpec((1, mq, LANES), lambda b, p, i: (b, i, p)),
        out_shape=jax.ShapeDtypeStruct((B, T, QA_W), _BF16),
        compiler_params=pltpu.CompilerParams(
            dimension_semantics=("arbitrary", "arbitrary", "arbitrary"),
            vmem_limit_bytes=VMEM_LIMIT),
        name="natten",
    )(qa, ka, va, bias)


def _dilated_kernel(q0_ref, k0_ref, v0_ref, q1_ref, k1_ref, v1_ref, q2_ref, k2_ref, v2_ref,
                    bias_ref, o_ref, so_ref, sm_ref, sl_ref):
    j = pl.program_id(2)
    chunk = o_ref.shape[1]
    refs = ((q0_ref, k0_ref, v0_ref), (q1_ref, k1_ref, v1_ref), (q2_ref, k2_ref, v2_ref))
    for g, dil in enumerate(DILATIONS_B):
        q_ref, k_ref, v_ref = refs[g]
        sub_len = k_ref.shape[2]
        per_res = chunk // dil
        nsb = per_res // QBLK_B

        def tile(n, carry, g=g, dil=dil, q_ref=q_ref, k_ref=k_ref, v_ref=v_ref,
                 sub_len=sub_len, per_res=per_res, nsb=nsb):
            r = n // nsb
            sb = n % nsb
            lq0 = j * per_res + sb * QBLK_B
            kl0 = jnp.clip(lq0 - HALF_KEYS, 0, sub_len - KBLK_B)
            variant = (lq0 - kl0) // HALF_KEYS
            kl0 = pl.multiple_of(kl0, HALF_KEYS)
            q = q_ref[0, r, pl.ds(pl.multiple_of(sb * QBLK_B, QBLK_B), QBLK_B), :]
            kw = k_ref[0, r, pl.ds(kl0, KBLK_B), :]
            vw = v_ref[0, r, pl.ds(kl0, KBLK_B), :]
            o, m, l = _pair_attention(q, kw, vw, lambda hh: bias_ref[0, g, variant, hh])
            if dil == 1:
                rows_out = pl.ds(pl.multiple_of(sb * QBLK_B, QBLK_B), QBLK_B)
            else:
                rows_out = pl.ds(sb * QBLK_B * dil + r, QBLK_B, stride=dil)
            so_ref[g, rows_out, :] = o
            sm_ref[g, rows_out, :] = m
            sl_ref[g, rows_out, :] = l
            return carry

        lax.fori_loop(0, dil * nsb, tile, 0)

    m_all = jnp.maximum(jnp.maximum(sm_ref[0], sm_ref[1]), sm_ref[2])
    num = jnp.zeros((chunk, LANES), _F32)
    den = jnp.zeros((chunk, LANES), _F32)
    for g in range(N_GROUPS_B):
        w = jnp.exp(sm_ref[g] - m_all)
        num = num + w * so_ref[g]
        den = den + w * sl_ref[g]
    o_ref[0] = (num / den).astype(_BF16)


def _t5_buckets(rel):
    half = NUM_BUCKETS // 2
    ret = np.where(rel > 0, half, 0)
    n = np.abs(rel)
    max_exact = half // 2
    large = max_exact + (np.log(np.maximum(n, 1) / max_exact)
                         / np.log(T5_MAX_DIST / max_exact) * (half - max_exact)).astype(np.int32)
    large = np.minimum(large, half - 1)
    return (ret + np.where(n < max_exact, n, large)).astype(np.int32)


def _dilated_bias(t5_bias):
    a = np.arange(QBLK_B)[:, None]
    b = np.arange(KBLK_B)[None, :]
    idx, ok = [], []
    for dil in DILATIONS_B:
        for variant in range(3):
            rel = b - a - variant * HALF_KEYS
            ok.append(np.abs(rel) <= HALF_KEYS)
            idx.append(_t5_buckets(np.clip(rel, -HALF_KEYS, HALF_KEYS) * dil))
    idx = np.stack(idx).reshape(N_GROUPS_B, 3, QBLK_B, KBLK_B)
    ok = np.stack(ok).reshape(N_GROUPS_B, 3, QBLK_B, KBLK_B)
    tab = t5_bias.astype(_F32)
    per_group = []
    for g in range(N_GROUPS_B):
        vals = jnp.take(tab[:, g * HG_B:(g + 1) * HG_B], jnp.asarray(idx[g].reshape(-1)), axis=0)
        vals = vals.reshape(3, QBLK_B, KBLK_B, HG_B)
        vals = jnp.where(jnp.asarray(ok[g])[..., None], vals, NEG)
        per_group.append(jnp.transpose(vals, (3, 0, 1, 2)))
    allg = jnp.stack(per_group)
    allg = allg.reshape(N_GROUPS_B, HG_B // 2, 2, 3, QBLK_B, KBLK_B)
    return jnp.transpose(allg, (1, 0, 3, 2, 4, 5))


def _dilated_call(b0, b1, b2, bias):
    B = b0.shape[0]
    T = b0.shape[2]
    chunk = min(CHUNK_B, T)
    assert T % chunk == 0 and chunk % (QBLK_B * max(DILATIONS_B)) == 0
    assert T // max(DILATIONS_B) >= KBLK_B
    in_specs, args = [], []
    for arr, dil in zip((b0, b1, b2), DILATIONS_B):
        sub_len = T // dil
        in_specs += [
            pl.BlockSpec((1, dil, chunk // dil, LANES), lambda b, p, j: (b, 0, j, p)),
            pl.BlockSpec((1, dil, sub_len, LANES), lambda b, p, j: (b, 0, 0, 2 + p)),
            pl.BlockSpec((1, dil, sub_len, LANES), lambda b, p, j: (b, 0, 0, 4 + p)),
        ]
        args += [arr, arr, arr]
    in_specs.append(pl.BlockSpec((1, N_GROUPS_B, 3, 2, QBLK_B, KBLK_B),
                                 lambda b, p, j: (p, 0, 0, 0, 0, 0)))
    args.append(bias)
    return pl.pallas_call(
        _dilated_kernel,
        grid=(B, HG_B // 2, T // chunk),
        in_specs=in_specs,
        out_specs=pl.BlockSpec((1, chunk, LANES), lambda b, p, j: (b, j, p)),
        out_shape=jax.ShapeDtypeStruct((B, T, GB_W), _BF16),
        scratch_shapes=[pltpu.VMEM((N_GROUPS_B, chunk, LANES), _F32)] * 3,
        compiler_params=pltpu.CompilerParams(
            dimension_semantics=("arbitrary", "arbitrary", "arbitrary"),
            vmem_limit_bytes=VMEM_LIMIT),
        name="dilated",
    )(*args)


def _mlp_kernel(x_ref, oa_ref, ob_ref, gate_ref, wa_ref, wb_ref, wo_ref, gmlp_ref, wup_ref,
                wdn_ref, y_ref):
    ya = jnp.dot(oa_ref[...], wa_ref[...], preferred_element_type=_F32)
    yb = jnp.dot(ob_ref[...], wb_ref[...], preferred_element_type=_F32)
    ga = gate_ref[:, :D_MODEL].astype(_F32)
    gb = gate_ref[:, D_MODEL:].astype(_F32)
    merged = (ga * ya + gb * yb).astype(_BF16)
    x1 = x_ref[...] + jnp.dot(merged, wo_ref[...], preferred_element_type=_F32)
    ms = jnp.mean(x1 * x1, axis=-1, keepdims=True)
    hm = (x1 * lax.rsqrt(ms + EPS) * gmlp_ref[...]).astype(_BF16)
    acc = x1
    for c in range(D_FF // FF_CHUNK):
        sl = slice(c * FF_CHUNK, (c + 1) * FF_CHUNK)
        u = jnp.maximum(jnp.dot(hm, wup_ref[:, sl], preferred_element_type=_F32), 0.0)
        acc = acc + jnp.dot((u * u).astype(_BF16), wdn_ref[sl, :], preferred_element_type=_F32)
    y_ref[...] = acc


def _mlp_call(x2, oa2, ob2, gates2, wa, wb, wo, gmlp, wup, wdn):
    n_tok, D = x2.shape
    tm = TM_MLP
    assert n_tok % tm == 0
    tok = lambda w: pl.BlockSpec((tm, w), lambda i: (i, 0))
    return pl.pallas_call(
        _mlp_kernel,
        grid=(n_tok // tm,),
        in_specs=[tok(D), tok(QA_W), tok(GB_W), tok(2 * D_MODEL),
                  _resident((QA_W, D)), _resident((GB_W, D)), _resident((D, D)),
                  _resident((1, D)), _resident((D, D_FF)), _resident((D_FF, D))],
        out_specs=tok(D),
        out_shape=jax.ShapeDtypeStruct((n_tok, D), _F32),
        compiler_params=pltpu.CompilerParams(
            dimension_semantics=("arbitrary",), vmem_limit_bytes=VMEM_LIMIT),
        name="mlp",
    )(x2, oa2, ob2, gates2, wa, wb, wo, gmlp, wup, wdn)


def _encoder_layer(x, p):
    B, T, D = x.shape
    qa, ka, va, b0, b1, b2, gates = _proj_call(
        x, p["gmix"], p["w_in"], p["gqa"], p["gka"], p["gqb"], p["gkb"], p["bd"])
    oa = _natten_call(qa, ka, va, p["bias_a"])
    ob = _dilated_call(b0, b1, b2, p["bias_b"])
    y = _mlp_call(x.reshape(B * T, D), oa.reshape(B * T, QA_W), ob.reshape(B * T, GB_W),
                  gates.reshape(B * T, 2 * D_MODEL), p["wa"], p["wb"], p["wo"], p["gmlp"],
                  p["wup"], p["wdn"])
    return y.reshape(B, T, D)


def _layer_params(norm_mix, w_in, q_norm_a, k_norm_a, q_norm_b, k_norm_b, rpb_a, t5_bias,
                  w_branch_a, w_branch_b, w_out, norm_mlp, w_up, w_down):
    tile_gain = lambda g: jnp.tile(g.astype(_F32), NORM_CHUNK // HEAD_DIM).reshape(1, NORM_CHUNK)
    head = np.arange(NORM_CHUNK) // HEAD_DIM
    bd = jnp.asarray((head[:, None] == head[None, :]).astype(np.float32) / HEAD_DIM, _BF16)
    return dict(
        gmix=norm_mix.astype(_F32).reshape(1, D_MODEL), w_in=w_in.astype(_BF16),
        gqa=tile_gain(q_norm_a), gka=tile_gain(k_norm_a),
        gqb=tile_gain(q_norm_b), gkb=tile_gain(k_norm_b), bd=bd,
        bias_a=_natten_bias(rpb_a), bias_b=_dilated_bias(t5_bias),
        wa=w_branch_a.astype(_BF16), wb=w_branch_b.astype(_BF16), wo=w_out.astype(_BF16),
        gmlp=norm_mlp.astype(_F32).reshape(1, D_MODEL),
        wup=w_up.astype(_BF16), wdn=w_down.astype(_BF16))


def kernel(x_prompt, x_sample, norm_mix, w_in, q_norm_a, k_norm_a, q_norm_b, k_norm_b, rpb_a,
           t5_bias, w_branch_a, w_branch_b, w_out, norm_mlp, w_up, w_down):
    y_prompt, y_sample = x_prompt, x_sample
    for l in range(norm_mix.shape[0]):
        p = _layer_params(norm_mix[l], w_in[l], q_norm_a[l], k_norm_a[l], q_norm_b[l],
                          k_norm_b[l], rpb_a[l], t5_bias, w_branch_a[l], w_branch_b[l],
                          w_out[l], norm_mlp[l], w_up[l], w_down[l])
        y_prompt = _encoder_layer(y_prompt, p)
        y_sample = _encoder_layer(y_sample, p)
    return (y_prompt, y_sample)
```

```python
import functools
import math

import numpy as np
import jax
import jax.numpy as jnp
from jax import lax
from jax.experimental import pallas as pl
from jax.experimental.pallas import tpu as pltpu

D_MODEL = 1024
HEAD_DIM = 64
H_A = 8
N_GROUPS_B = 3
HG_B = 4
H_B = N_GROUPS_B * HG_B
WINDOWS_B = (128, 512, 2048)
DILATIONS_B = (1, 4, 16)
GRID_W = 64
WIN_ROWS = 8
WIN_COLS = 16
NUM_BUCKETS = 32
T5_MAX_DIST = 1024
D_FF = 4 * D_MODEL
EPS = 1e-6
NEG = -1e30
QA_W = H_A * HEAD_DIM
QB_W = H_B * HEAD_DIM
GB_W = HG_B * HEAD_DIM
IN_W = 3 * QA_W + 3 * QB_W + 2 * D_MODEL
HALF_KEYS = 64
assert all((w // 2) // d == HALF_KEYS for w, d in zip(WINDOWS_B, DILATIONS_B))

LANES = 128
NORM_CHUNK = 256
TM_PROJ = 512
TM_MLP = 512
FF_CHUNK = 1024
QROWS_A = 4
KROWS_A = QROWS_A + 8
QBLK_B = 128
KBLK_B = QBLK_B + 2 * HALF_KEYS
CHUNK_B = 2048
VMEM_LIMIT = 56 * 1024 * 1024

_F32 = jnp.float32
_BF16 = jnp.bfloat16


def _resident(shape):
    nd = len(shape)
    return pl.BlockSpec(shape, lambda *_: (0,) * nd, pipeline_mode=pl.Buffered(1))


def _proj_kernel(x_ref, gmix_ref, w_ref, gqa_ref, gka_ref, gqb_ref, gkb_ref, bd_ref,
                 qa_ref, ka_ref, va_ref, b0_ref, b1_ref, b2_ref, gate_ref, scr_ref):
    tm = x_ref.shape[1]
    x = x_ref[0]
    ms = jnp.mean(x * x, axis=-1, keepdims=True)
    h = (x * lax.rsqrt(ms + EPS) * gmix_ref[...]).astype(_BF16)

    def proj(c0, width):
        return jnp.dot(h, w_ref[:, c0:c0 + width], preferred_element_type=_F32)

    def head_norm(t, gain_ref, scale):
        msq = jnp.dot((t * t).astype(_BF16), bd_ref[...], preferred_element_type=_F32)
        return t * lax.rsqrt(msq + EPS) * (gain_ref[...] * scale)

    qk_scale = HEAD_DIM ** -0.5

    t = proj(0, QA_W)
    for c in range(QA_W // NORM_CHUNK):
        sl = slice(c * NORM_CHUNK, (c + 1) * NORM_CHUNK)
        qa_ref[0, :, sl] = head_norm(t[:, sl], gqa_ref, qk_scale).astype(_BF16)
    t = proj(QA_W, QA_W)
    for c in range(QA_W // NORM_CHUNK):
        sl = slice(c * NORM_CHUNK, (c + 1) * NORM_CHUNK)
        ka_ref[0, :, sl] = head_norm(t[:, sl], gka_ref, 1.0).astype(_BF16)
    va_ref[0] = proj(2 * QA_W, QA_W).astype(_BF16)

    outs = (b0_ref, b1_ref, b2_ref)
    base = 3 * QA_W
    slab = 0
    for kind in range(3):
        t = proj(base + kind * QB_W, QB_W)
        for g, dil in enumerate(DILATIONS_B):
            tg = t[:, g * GB_W:(g + 1) * GB_W]
            if kind == 0:
                tg = head_norm(tg, gqb_ref, qk_scale)
            elif kind == 1:
                tg = head_norm(tg, gkb_ref, 1.0)
            lane0 = kind * GB_W
            if dil == 1:
                outs[g][0, 0, :, lane0:lane0 + GB_W] = tg.astype(_BF16)
                continue
            for half in range(GB_W // LANES):
                scr_ref[slab] = tg[:, half * LANES:(half + 1) * LANES]
                for r in range(dil):
                    sub = scr_ref[slab, pl.ds(r, tm // dil, stride=dil), :]
                    l0 = lane0 + half * LANES
                    outs[g][0, r, :, l0:l0 + LANES] = sub.astype(_BF16)
                slab += 1

    base = 3 * QA_W + 3 * QB_W
    gate_ref[0] = jax.nn.sigmoid(proj(base, 2 * D_MODEL)).astype(_BF16)


def _proj_call(x, gmix, w_in, gqa, gka, gqb, gkb, bd):
    B, T, D = x.shape
    tm = TM_PROJ
    assert T % tm == 0 and tm % (16 * max(DILATIONS_B)) == 0
    n_slabs = 3 * sum(1 for d in DILATIONS_B if d > 1) * (GB_W // LANES)
    tok = lambda w: pl.BlockSpec((1, tm, w), lambda b, i: (b, i, 0))
    sub = lambda d: pl.BlockSpec((1, d, tm // d, 3 * GB_W), lambda b, i: (b, 0, i, 0))
    out_shape = (
        jax.ShapeDtypeStruct((B, T, QA_W), _BF16),
        jax.ShapeDtypeStruct((B, T, QA_W), _BF16),
        jax.ShapeDtypeStruct((B, T, QA_W), _BF16),
    ) + tuple(jax.ShapeDtypeStruct((B, d, T // d, 3 * GB_W), _BF16) for d in DILATIONS_B) + (
        jax.ShapeDtypeStruct((B, T, 2 * D_MODEL), _BF16),
    )
    return pl.pallas_call(
        _proj_kernel,
        grid=(B, T // tm),
        in_specs=[tok(D), _resident((1, D)), _resident((D, IN_W)),
                  _resident((1, NORM_CHUNK)), _resident((1, NORM_CHUNK)),
                  _resident((1, NORM_CHUNK)), _resident((1, NORM_CHUNK)),
                  _resident((NORM_CHUNK, NORM_CHUNK))],
        out_specs=(tok(QA_W), tok(QA_W), tok(QA_W)) + tuple(sub(d) for d in DILATIONS_B)
        + (tok(2 * D_MODEL),),
        out_shape=out_shape,
        scratch_shapes=[pltpu.VMEM((n_slabs, tm, LANES), _F32)],
        compiler_params=pltpu.CompilerParams(
            dimension_semantics=("arbitrary", "arbitrary"), vmem_limit_bytes=VMEM_LIMIT),
        name="proj",
    )(x, gmix, w_in, gqa, gka, gqb, gkb, bd)


def _pair_attention(q, kw, vw, bias_of_head):
    m_rows = q.shape[0]
    lane = lax.broadcasted_iota(jnp.int32, (m_rows, LANES), 1)
    first = lane < HEAD_DIM
    o_h, m_h, l_h = [], [], []
    for hh in range(2):
        qh = jnp.where(first if hh == 0 else jnp.logical_not(first), q, jnp.zeros_like(q))
        s = lax.dot_general(qh, kw, (((1,), (1,)), ((), ())), preferred_element_type=_F32)
        s = s + bias_of_head(hh)
        m = jnp.max(s, axis=-1, keepdims=True)
        p = jnp.exp(s - m)
        l = jnp.sum(p, axis=-1, keepdims=True)
        o_h.append(jnp.dot(p.astype(_BF16), vw, preferred_element_type=_F32))
        m_h.append(jnp.broadcast_to(m, (m_rows, LANES)))
        l_h.append(jnp.broadcast_to(l, (m_rows, LANES)))
    pick = lambda pair: jnp.where(first, pair[0], pair[1])
    return pick(o_h), pick(m_h), pick(l_h)


def _toeplitz(v, n_rows, n_cols, center):
    n = v.shape[-1]
    period = n_rows + n_cols - 1
    left = n_rows - 1 - center
    cfg = [(0, 0, 0)] * (v.ndim - 1) + [(left, period - n - left, 0)]
    w = lax.pad(v, jnp.asarray(NEG, v.dtype), cfg)
    flat = jnp.tile(w, (1,) * (v.ndim - 1) + (n_rows + 1,))[..., :n_rows * (period + 1)]
    shifted = flat.reshape(v.shape[:-1] + (n_rows, period + 1))[..., :n_cols]
    return jnp.flip(shifted, axis=-2)


def _natten_kernel(q_ref, k_ref, v_ref, bias_ref, o_ref, *, rows):
    i = pl.program_id(2)
    krow0 = jnp.clip(QROWS_A * i - WIN_ROWS // 2, 0, rows - KROWS_A)
    k0 = pl.multiple_of(krow0 * GRID_W, GRID_W)
    nk = KROWS_A * GRID_W
    kw = k_ref[0, pl.ds(k0, nk), :]
    vw = v_ref[0, pl.ds(k0, nk), :]
    o, _, l = _pair_attention(q_ref[0], kw, vw, lambda hh: bias_ref[0, hh])
    o_ref[0] = (o / l).astype(_BF16)


def _natten_bias(rpb):
    n_off = 2 * WIN_ROWS - 1
    toep = _toeplitz(rpb.astype(_F32), GRID_W, GRID_W, WIN_COLS - 1)
    c = np.arange(GRID_W)[:, None]
    kc = np.arange(GRID_W)[None, :]
    col_start = np.clip(c - WIN_COLS // 2, 0, GRID_W - WIN_COLS)
    col_ok = (kc >= col_start) & (kc < col_start + WIN_COLS)
    col_bias = jnp.where(jnp.asarray(col_ok), toep, NEG)
    neg_block = jnp.full((H_A, GRID_W, GRID_W), NEG, _F32)
    fake_rows = 3 * KROWS_A
    nblk = fake_rows // QROWS_A
    variants = []
    for blk in (0, nblk // 2, nblk - 1):
        krow0 = int(np.clip(QROWS_A * blk - WIN_ROWS // 2, 0, fake_rows - KROWS_A))
        q_rows = []
        for rq in range(QROWS_A):
            r = QROWS_A * blk + rq
            start = int(np.clip(r - WIN_ROWS // 2, 0, fake_rows - WIN_ROWS))
            blocks = []
            for rk in range(KROWS_A):
                kr = krow0 + rk
                row_off = kr - r + WIN_ROWS - 1
                inside = start <= kr < start + WIN_ROWS
                assert not inside or 0 <= row_off < n_off
                blocks.append(col_bias[:, row_off] if inside else neg_block)
            q_rows.append(jnp.concatenate(blocks, axis=-1))
        variants.append(jnp.concatenate(q_rows, axis=-2))
    return jnp.stack(variants)


def _natten_call(qa, ka, va, bias):
    B, T, _ = qa.shape
    rows = T // GRID_W
    assert T % GRID_W == 0 and rows % QROWS_A == 0 and rows >= 3 * KROWS_A // 2
    nblk = rows // QROWS_A
    mq, nk = QROWS_A * GRID_W, KROWS_A * GRID_W
    variant = lambda i: jnp.where(i == 0, 0, jnp.where(i == nblk - 1, 2, 1))
    return pl.pallas_call(
        functools.partial(_natten_kernel, rows=rows),
        grid=(B, H_A // 2, nblk),
        in_specs=[pl.BlockSpec((1, mq, LANES), lambda b, p, i: (b, i, p)),
                  pl.BlockSpec((1, T, LANES), lambda b, p, i: (b, 0, p)),
                  pl.BlockSpec((1, T, LANES), lambda b, p, i: (b, 0, p)),
                  pl.BlockSpec((1, 2, mq, nk), lambda b, p, i: (variant(i), p, 0, 0))],
        out_specs=pl.BlockSpec((1, mq, LANES), lambda b, p, i: (b, i, p)),
        out_shape=jax.ShapeDtypeStruct((B, T, QA_W), _BF16),
        compiler_params=pltpu.CompilerParams(
            dimension_semantics=("arbitrary", "arbitrary", "arbitrary"),
            vmem_limit_bytes=VMEM_LIMIT),
        name="natten",
    )(qa, ka, va, bias)


def _dilated_kernel(q0_ref, k0_ref, v0_ref, q1_ref, k1_ref, v1_ref, q2_ref, k2_ref, v2_ref,
                    bias_ref, o_ref, so_ref, sm_ref, sl_ref):
    j = pl.program_id(2)
    chunk = o_ref.shape[1]
    refs = ((q0_ref, k0_ref, v0_ref), (q1_ref, k1_ref, v1_ref), (q2_ref, k2_ref, v2_ref))
    for g, dil in enumerate(DILATIONS_B):
        q_ref, k_ref, v_ref = refs[g]
        sub_len = k_ref.shape[2]
        per_res = chunk // dil
        nsb = per_res // QBLK_B

        def tile(n, carry, g=g, dil=dil, q_ref=q_ref, k_ref=k_ref, v_ref=v_ref,
                 sub_len=sub_len, per_res=per_res, nsb=nsb):
            r = n // nsb
            sb = n % nsb
            lq0 = j * per_res + sb * QBLK_B
            kl0 = jnp.clip(lq0 - HALF_KEYS, 0, sub_len - KBLK_B)
            variant = (lq0 - kl0) // HALF_KEYS
            kl0 = pl.multiple_of(kl0, HALF_KEYS)
            q = q_ref[0, r, pl.ds(pl.multiple_of(sb * QBLK_B, QBLK_B), QBLK_B), :]
            kw = k_ref[0, r, pl.ds(kl0, KBLK_B), :]
            vw = v_ref[0, r, pl.ds(kl0, KBLK_B), :]
            o, m, l = _pair_attention(q, kw, vw, lambda hh: bias_ref[0, g, variant, hh])
            if dil == 1:
                rows_out = pl.ds(pl.multiple_of(sb * QBLK_B, QBLK_B), QBLK_B)
            else:
                rows_out = pl.ds(sb * QBLK_B * dil + r, QBLK_B, stride=dil)
            so_ref[g, rows_out, :] = o
            sm_ref[g, rows_out, :] = m
            sl_ref[g, rows_out, :] = l
            return carry

        lax.fori_loop(0, dil * nsb, tile, 0)

    m_all = jnp.maximum(jnp.maximum(sm_ref[0], sm_ref[1]), sm_ref[2])
    num = jnp.zeros((chunk, LANES), _F32)
    den = jnp.zeros((chunk, LANES), _F32)
    for g in range(N_GROUPS_B):
        w = jnp.exp(sm_ref[g] - m_all)
        num = num + w * so_ref[g]
        den = den + w * sl_ref[g]
    o_ref[0] = (num / den).astype(_BF16)


def _t5_buckets(rel):
    half = NUM_BUCKETS // 2
    ret = np.where(rel > 0, half, 0)
    n = np.abs(rel)
    max_exact = half // 2
    large = max_exact + (np.log(np.maximum(n, 1) / max_exact)
                         / np.log(T5_MAX_DIST / max_exact) * (half - max_exact)).astype(np.int32)
    large = np.minimum(large, half - 1)
    return (ret + np.where(n < max_exact, n, large)).astype(np.int32)


def _dilated_bias(t5_bias):
    rel = np.arange(-HALF_KEYS, HALF_KEYS + 1)
    tab = t5_bias.astype(_F32).T
    by_rel = jnp.stack([
        jnp.take(tab[g * HG_B:(g + 1) * HG_B], jnp.asarray(_t5_buckets(rel * dil)), axis=1)
        for g, dil in enumerate(DILATIONS_B)])
    tiles = jnp.stack([
        _toeplitz(by_rel, QBLK_B, KBLK_B, HALF_KEYS - variant * HALF_KEYS)
        for variant in range(3)])
    tiles = tiles.reshape(3, N_GROUPS_B, HG_B // 2, 2, QBLK_B, KBLK_B)
    return jnp.transpose(tiles, (2, 1, 0, 3, 4, 5))


def _dilated_call(b0, b1, b2, bias):
    B = b0.shape[0]
    T = b0.shape[2]
    chunk = min(CHUNK_B, T)
    assert T % chunk == 0 and chunk % (QBLK_B * max(DILATIONS_B)) == 0
    assert T // max(DILATIONS_B) >= KBLK_B
    in_specs, args = [], []
    for arr, dil in zip((b0, b1, b2), DILATIONS_B):
        sub_len = T // dil
        in_specs += [
            pl.BlockSpec((1, dil, chunk // dil, LANES), lambda b, p, j: (b, 0, j, p)),
            pl.BlockSpec((1, dil, sub_len, LANES), lambda b, p, j: (b, 0, 0, 2 + p)),
            pl.BlockSpec((1, dil, sub_len, LANES), lambda b, p, j: (b, 0, 0, 4 + p)),
        ]
        args += [arr, arr, arr]
    in_specs.append(pl.BlockSpec((1, N_GROUPS_B, 3, 2, QBLK_B, KBLK_B),
                                 lambda b, p, j: (p, 0, 0, 0, 0, 0)))
    args.append(bias)
    return pl.pallas_call(
        _dilated_kernel,
        grid=(B, HG_B // 2, T // chunk),
        in_specs=in_specs,
        out_specs=pl.BlockSpec((1, chunk, LANES), lambda b, p, j: (b, j, p)),
        out_shape=jax.ShapeDtypeStruct((B, T, GB_W), _BF16),
        scratch_shapes=[pltpu.VMEM((N_GROUPS_B, chunk, LANES), _F32)] * 3,
        compiler_params=pltpu.CompilerParams(
            dimension_semantics=("arbitrary", "arbitrary", "arbitrary"),
            vmem_limit_bytes=VMEM_LIMIT),
        name="dilated",
    )(*args)


def _mlp_kernel(x_ref, oa_ref, ob_ref, gate_ref, wa_ref, wb_ref, wo_ref, gmlp_ref, wup_ref,
                wdn_ref, y_ref):
    ya = jnp.dot(oa_ref[...], wa_ref[...], preferred_element_type=_F32)
    yb = jnp.dot(ob_ref[...], wb_ref[...], preferred_element_type=_F32)
    ga = gate_ref[:, :D_MODEL].astype(_F32)
    gb = gate_ref[:, D_MODEL:].astype(_F32)
    merged = (ga * ya + gb * yb).astype(_BF16)
    x1 = x_ref[...] + jnp.dot(merged, wo_ref[...], preferred_element_type=_F32)
    ms = jnp.mean(x1 * x1, axis=-1, keepdims=True)
    hm = (x1 * lax.rsqrt(ms + EPS) * gmlp_ref[...]).astype(_BF16)
    acc = x1
    for c in range(D_FF // FF_CHUNK):
        sl = slice(c * FF_CHUNK, (c + 1) * FF_CHUNK)
        u = jnp.maximum(jnp.dot(hm, wup_ref[:, sl], preferred_element_type=_F32), 0.0)
        acc = acc + jnp.dot((u * u).astype(_BF16), wdn_ref[sl, :], preferred_element_type=_F32)
    y_ref[...] = acc


def _mlp_call(x2, oa2, ob2, gates2, wa, wb, wo, gmlp, wup, wdn):
    n_tok, D = x2.shape
    tm = TM_MLP
    assert n_tok % tm == 0
    tok = lambda w: pl.BlockSpec((tm, w), lambda i: (i, 0))
    return pl.pallas_call(
        _mlp_kernel,
        grid=(n_tok // tm,),
        in_specs=[tok(D), tok(QA_W), tok(GB_W), tok(2 * D_MODEL),
                  _resident((QA_W, D)), _resident((GB_W, D)), _resident((D, D)),
                  _resident((1, D)), _resident((D, D_FF)), _resident((D_FF, D))],
        out_specs=tok(D),
        out_shape=jax.ShapeDtypeStruct((n_tok, D), _F32),
        compiler_params=pltpu.CompilerParams(
            dimension_semantics=("arbitrary",), vmem_limit_bytes=VMEM_LIMIT),
        name="mlp",
    )(x2, oa2, ob2, gates2, wa, wb, wo, gmlp, wup, wdn)


def _encoder_layer(x, p):
    B, T, D = x.shape
    qa, ka, va, b0, b1, b2, gates = _proj_call(
        x, p["gmix"], p["w_in"], p["gqa"], p["gka"], p["gqb"], p["gkb"], p["bd"])
    oa = _natten_call(qa, ka, va, p["bias_a"])
    ob = _dilated_call(b0, b1, b2, p["bias_b"])
    y = _mlp_call(x.reshape(B * T, D), oa.reshape(B * T, QA_W), ob.reshape(B * T, GB_W),
                  gates.reshape(B * T, 2 * D_MODEL), p["wa"], p["wb"], p["wo"], p["gmlp"],
                  p["wup"], p["wdn"])
    return y.reshape(B, T, D)


def _layer_params(norm_mix, w_in, q_norm_a, k_norm_a, q_norm_b, k_norm_b, rpb_a, t5_bias,
                  w_branch_a, w_branch_b, w_out, norm_mlp, w_up, w_down):
    tile_gain = lambda g: jnp.tile(g.astype(_F32), NORM_CHUNK // HEAD_DIM).reshape(1, NORM_CHUNK)
    head = np.arange(NORM_CHUNK) // HEAD_DIM
    bd = jnp.asarray((head[:, None] == head[None, :]).astype(np.float32) / HEAD_DIM, _BF16)
    return dict(
        gmix=norm_mix.astype(_F32).reshape(1, D_MODEL), w_in=w_in.astype(_BF16),
        gqa=tile_gain(q_norm_a), gka=tile_gain(k_norm_a),
        gqb=tile_gain(q_norm_b), gkb=tile_gain(k_norm_b), bd=bd,
        bias_a=_natten_bias(rpb_a), bias_b=_dilated_bias(t5_bias),
        wa=w_branch_a.astype(_BF16), wb=w_branch_b.astype(_BF16), wo=w_out.astype(_BF16),
        gmlp=norm_mlp.astype(_F32).reshape(1, D_MODEL),
        wup=w_up.astype(_BF16), wdn=w_down.astype(_BF16))


def kernel(x_prompt, x_sample, norm_mix, w_in, q_norm_a, k_norm_a, q_norm_b, k_norm_b, rpb_a,
           t5_bias, w_branch_a, w_branch_b, w_out, norm_mlp, w_up, w_down):
    y_prompt, y_sample = x_prompt, x_sample
    for l in range(norm_mix.shape[0]):
        p = _layer_params(norm_mix[l], w_in[l], q_norm_a[l], k_norm_a[l], q_norm_b[l],
                          k_norm_b[l], rpb_a[l], t5_bias, w_branch_a[l], w_branch_b[l],
                          w_out[l], norm_mlp[l], w_up[l], w_down[l])
        y_prompt = _encoder_layer(y_prompt, p)
        y_sample = _encoder_layer(y_sample, p)
    return (y_prompt, y_sample)
```

```python
import functools
import math

import numpy as np
import jax
import jax.numpy as jnp
from jax import lax
from jax.experimental import pallas as pl
from jax.experimental.pallas import tpu as pltpu

D_MODEL = 1024
HEAD_DIM = 64
H_A = 8
N_GROUPS_B = 3
HG_B = 4
H_B = N_GROUPS_B * HG_B
WINDOWS_B = (128, 512, 2048)
DILATIONS_B = (1, 4, 16)
GRID_W = 64
WIN_ROWS = 8
WIN_COLS = 16
NUM_BUCKETS = 32
T5_MAX_DIST = 1024
D_FF = 4 * D_MODEL
EPS = 1e-6
NEG = -1e30
QA_W = H_A * HEAD_DIM
QB_W = H_B * HEAD_DIM
GB_W = HG_B * HEAD_DIM
IN_W = 3 * QA_W + 3 * QB_W + 2 * D_MODEL
HALF_KEYS = 64
assert all((w // 2) // d == HALF_KEYS for w, d in zip(WINDOWS_B, DILATIONS_B))

LANES = 128
NORM_CHUNK = 256
TM_PROJ = 512
TM_MLP = 512
FF_CHUNK = 1024
QROWS_A = 4
KROWS_A = QROWS_A + 8
BLOCKS_PER_STEP_A = 16
QBLK_B = 128
KBLK_B = QBLK_B + 2 * HALF_KEYS
CHUNK_B = 2048
TILE_UNROLL_B = 16
VMEM_LIMIT = 56 * 1024 * 1024

_F32 = jnp.float32
_BF16 = jnp.bfloat16


def _resident(shape):
    nd = len(shape)
    return pl.BlockSpec(shape, lambda *_: (0,) * nd, pipeline_mode=pl.Buffered(1))


def _proj_kernel(x_ref, gmix_ref, w_ref, gqa_ref, gka_ref, gqb_ref, gkb_ref, bd_ref,
                 qa_ref, ka_ref, va_ref, b0_ref, b1_ref, b2_ref, gate_ref, scr_ref):
    tm = x_ref.shape[1]
    x = x_ref[0]
    ms = jnp.mean(x * x, axis=-1, keepdims=True)
    h = (x * lax.rsqrt(ms + EPS) * gmix_ref[...]).astype(_BF16)

    def proj(c0, width):
        return jnp.dot(h, w_ref[:, c0:c0 + width], preferred_element_type=_F32)

    def head_norm(t, gain_ref, scale):
        msq = jnp.dot((t * t).astype(_BF16), bd_ref[...], preferred_element_type=_F32)
        return t * lax.rsqrt(msq + EPS) * (gain_ref[...] * scale)

    qk_scale = HEAD_DIM ** -0.5

    t = proj(0, QA_W)
    for c in range(QA_W // NORM_CHUNK):
        sl = slice(c * NORM_CHUNK, (c + 1) * NORM_CHUNK)
        qa_ref[0, :, sl] = head_norm(t[:, sl], gqa_ref, qk_scale).astype(_BF16)
    t = proj(QA_W, QA_W)
    for c in range(QA_W // NORM_CHUNK):
        sl = slice(c * NORM_CHUNK, (c + 1) * NORM_CHUNK)
        ka_ref[0, :, sl] = head_norm(t[:, sl], gka_ref, 1.0).astype(_BF16)
    va_ref[0] = proj(2 * QA_W, QA_W).astype(_BF16)

    outs = (b0_ref, b1_ref, b2_ref)
    base = 3 * QA_W
    slab = 0
    for kind in range(3):
        t = proj(base + kind * QB_W, QB_W)
        for g, dil in enumerate(DILATIONS_B):
            tg = t[:, g * GB_W:(g + 1) * GB_W]
            if kind == 0:
                tg = head_norm(tg, gqb_ref, qk_scale)
            elif kind == 1:
                tg = head_norm(tg, gkb_ref, 1.0)
            lane0 = kind * GB_W
            if dil == 1:
                outs[g][0, 0, :, lane0:lane0 + GB_W] = tg.astype(_BF16)
                continue
            for half in range(GB_W // LANES):
                scr_ref[slab] = tg[:, half * LANES:(half + 1) * LANES]
                for r in range(dil):
                    sub = scr_ref[slab, pl.ds(r, tm // dil, stride=dil), :]
                    l0 = lane0 + half * LANES
                    outs[g][0, r, :, l0:l0 + LANES] = sub.astype(_BF16)
                slab += 1

    base = 3 * QA_W + 3 * QB_W
    gate_ref[0] = jax.nn.sigmoid(proj(base, 2 * D_MODEL)).astype(_BF16)


def _proj_call(x, gmix, w_in, gqa, gka, gqb, gkb, bd):
    B, T, D = x.shape
    tm = TM_PROJ
    assert T % tm == 0 and tm % (16 * max(DILATIONS_B)) == 0
    n_slabs = 3 * sum(1 for d in DILATIONS_B if d > 1) * (GB_W // LANES)
    tok = lambda w: pl.BlockSpec((1, tm, w), lambda b, i: (b, i, 0))
    sub = lambda d: pl.BlockSpec((1, d, tm // d, 3 * GB_W), lambda b, i: (b, 0, i, 0))
    out_shape = (
        jax.ShapeDtypeStruct((B, T, QA_W), _BF16),
        jax.ShapeDtypeStruct((B, T, QA_W), _BF16),
        jax.ShapeDtypeStruct((B, T, QA_W), _BF16),
    ) + tuple(jax.ShapeDtypeStruct((B, d, T // d, 3 * GB_W), _BF16) for d in DILATIONS_B) + (
        jax.ShapeDtypeStruct((B, T, 2 * D_MODEL), _BF16),
    )
    return pl.pallas_call(
        _proj_kernel,
        grid=(B, T // tm),
        in_specs=[tok(D), _resident((1, D)), _resident((D, IN_W)),
                  _resident((1, NORM_CHUNK)), _resident((1, NORM_CHUNK)),
                  _resident((1, NORM_CHUNK)), _resident((1, NORM_CHUNK)),
                  _resident((NORM_CHUNK, NORM_CHUNK))],
        out_specs=(tok(QA_W), tok(QA_W), tok(QA_W)) + tuple(sub(d) for d in DILATIONS_B)
        + (tok(2 * D_MODEL),),
        out_shape=out_shape,
        scratch_shapes=[pltpu.VMEM((n_slabs, tm, LANES), _F32)],
        compiler_params=pltpu.CompilerParams(
            dimension_semantics=("arbitrary", "arbitrary"), vmem_limit_bytes=VMEM_LIMIT),
        name="proj",
    )(x, gmix, w_in, gqa, gka, gqb, gkb, bd)


def _pair_attention(q, kw, vw, bias2):
    m_rows = q.shape[0]
    lane = lax.broadcasted_iota(jnp.int32, (m_rows, LANES), 1)
    first = lane < HEAD_DIM
    zero = jnp.zeros_like(q)
    q2 = jnp.concatenate([jnp.where(first, q, zero), jnp.where(first, zero, q)], axis=0)
    s = lax.dot_general(q2, kw, (((1,), (1,)), ((), ())), preferred_element_type=_F32)
    s = s + bias2
    m = jnp.max(s, axis=-1, keepdims=True)
    p = jnp.exp(s - m)
    l = jnp.sum(p, axis=-1, keepdims=True)
    o2 = jnp.dot(p.astype(_BF16), vw, preferred_element_type=_F32)

    def pick(x2):
        x2 = jnp.broadcast_to(x2, (2 * m_rows, LANES))
        return jnp.where(first, x2[:m_rows], x2[m_rows:])

    return pick(o2), pick(m), pick(l)


def _toeplitz(v, n_rows, n_cols, center):
    n = v.shape[-1]
    period = n_rows + n_cols - 1
    left = n_rows - 1 - center
    cfg = [(0, 0, 0)] * (v.ndim - 1) + [(left, period - n - left, 0)]
    w = lax.pad(v, jnp.asarray(NEG, v.dtype), cfg)
    flat = jnp.tile(w, (1,) * (v.ndim - 1) + (n_rows + 1,))[..., :n_rows * (period + 1)]
    shifted = flat.reshape(v.shape[:-1] + (n_rows, period + 1))[..., :n_cols]
    return jnp.flip(shifted, axis=-2)


def _natten_kernel(q_ref, k_ref, v_ref, bias_ref, o_ref, *, rows):
    i = pl.program_id(2)
    mq, nk = QROWS_A * GRID_W, KROWS_A * GRID_W
    last_blk = rows // QROWS_A - 1
    for u in range(BLOCKS_PER_STEP_A):
        blk = BLOCKS_PER_STEP_A * i + u
        krow0 = jnp.clip(QROWS_A * blk - WIN_ROWS // 2, 0, rows - KROWS_A)
        k0 = pl.multiple_of(krow0 * GRID_W, GRID_W)
        variant = jnp.where(blk == 0, 0, jnp.where(blk == last_blk, 2, 1))
        kw = k_ref[0, pl.ds(k0, nk), :]
        vw = v_ref[0, pl.ds(k0, nk), :]
        q = q_ref[0, u * mq:(u + 1) * mq, :]
        o, _, l = _pair_attention(q, kw, vw, bias_ref[variant, 0])
        o_ref[0, u * mq:(u + 1) * mq, :] = (o / l).astype(_BF16)


def _natten_bias(rpb):
    n_off = 2 * WIN_ROWS - 1
    toep = _toeplitz(rpb.astype(_F32), GRID_W, GRID_W, WIN_COLS - 1)
    c = np.arange(GRID_W)[:, None]
    kc = np.arange(GRID_W)[None, :]
    col_start = np.clip(c - WIN_COLS // 2, 0, GRID_W - WIN_COLS)
    col_ok = (kc >= col_start) & (kc < col_start + WIN_COLS)
    col_bias = jnp.where(jnp.asarray(col_ok), toep, NEG)
    neg_block = jnp.full((H_A, GRID_W, GRID_W), NEG, _F32)
    fake_rows = 3 * KROWS_A
    nblk = fake_rows // QROWS_A
    variants = []
    for blk in (0, nblk // 2, nblk - 1):
        krow0 = int(np.clip(QROWS_A * blk - WIN_ROWS // 2, 0, fake_rows - KROWS_A))
        q_rows = []
        for rq in range(QROWS_A):
            r = QROWS_A * blk + rq
            start = int(np.clip(r - WIN_ROWS // 2, 0, fake_rows - WIN_ROWS))
            blocks = []
            for rk in range(KROWS_A):
                kr = krow0 + rk
                row_off = kr - r + WIN_ROWS - 1
                inside = start <= kr < start + WIN_ROWS
                assert not inside or 0 <= row_off < n_off
                blocks.append(col_bias[:, row_off] if inside else neg_block)
            q_rows.append(jnp.concatenate(blocks, axis=-1))
        variants.append(jnp.concatenate(q_rows, axis=-2))
    tiles = jnp.stack(variants)
    return tiles.reshape(3, H_A // 2, 2 * QROWS_A * GRID_W, KROWS_A * GRID_W)


def _natten_call(qa, ka, va, bias):
    B, T, _ = qa.shape
    rows = T // GRID_W
    rows_per_step = QROWS_A * BLOCKS_PER_STEP_A
    assert T % GRID_W == 0 and rows % rows_per_step == 0 and rows >= KROWS_A
    mq, nk = rows_per_step * GRID_W, KROWS_A * GRID_W
    return pl.pallas_call(
        functools.partial(_natten_kernel, rows=rows),
        grid=(B, H_A // 2, rows // rows_per_step),
        in_specs=[pl.BlockSpec((1, mq, LANES), lambda b, p, i: (b, i, p)),
                  pl.BlockSpec((1, T, LANES), lambda b, p, i: (b, 0, p)),
                  pl.BlockSpec((1, T, LANES), lambda b, p, i: (b, 0, p)),
                  pl.BlockSpec((3, 1, 2 * QROWS_A * GRID_W, nk), lambda b, p, i: (0, p, 0, 0))],
        out_specs=pl.BlockSpec((1, mq, LANES), lambda b, p, i: (b, i, p)),
        out_shape=jax.ShapeDtypeStruct((B, T, QA_W), _BF16),
        compiler_params=pltpu.CompilerParams(
            dimension_semantics=("arbitrary", "arbitrary", "arbitrary"),
            vmem_limit_bytes=VMEM_LIMIT),
        name="natten",
    )(qa, ka, va, bias)


def _dilated_kernel(q0_ref, k0_ref, v0_ref, q1_ref, k1_ref, v1_ref, q2_ref, k2_ref, v2_ref,
                    bias_ref, o_ref, so_ref, sm_ref, sl_ref):
    j = pl.program_id(2)
    chunk = o_ref.shape[1]
    refs = ((q0_ref, k0_ref, v0_ref), (q1_ref, k1_ref, v1_ref), (q2_ref, k2_ref, v2_ref))
    for g, dil in enumerate(DILATIONS_B):
        q_ref, k_ref, v_ref = refs[g]
        sub_len = k_ref.shape[2]
        per_res = chunk // dil
        nsb = per_res // QBLK_B

        def tile(n, carry, g=g, dil=dil, q_ref=q_ref, k_ref=k_ref, v_ref=v_ref,
                 sub_len=sub_len, per_res=per_res, nsb=nsb):
            r = n // nsb
            sb = n % nsb
            lq0 = j * per_res + sb * QBLK_B
            kl0 = jnp.clip(lq0 - HALF_KEYS, 0, sub_len - KBLK_B)
            variant = (lq0 - kl0) // HALF_KEYS
            kl0 = pl.multiple_of(kl0, HALF_KEYS)
            q = q_ref[0, r, pl.ds(pl.multiple_of(sb * QBLK_B, QBLK_B), QBLK_B), :]
            kw = k_ref[0, r, pl.ds(kl0, KBLK_B), :]
            vw = v_ref[0, r, pl.ds(kl0, KBLK_B), :]
            o, m, l = _pair_attention(q, kw, vw, bias_ref[0, g, variant])
            if dil == 1:
                rows_out = pl.ds(pl.multiple_of(sb * QBLK_B, QBLK_B), QBLK_B)
            else:
                rows_out = pl.ds(sb * QBLK_B * dil + r, QBLK_B, stride=dil)
            so_ref[g, rows_out, :] = o
            sm_ref[g, rows_out, :] = m
            sl_ref[g, rows_out, :] = l
            return carry

        lax.fori_loop(0, dil * nsb, tile, 0, unroll=TILE_UNROLL_B)

    m_all = jnp.maximum(jnp.maximum(sm_ref[0], sm_ref[1]), sm_ref[2])
    num = jnp.zeros((chunk, LANES), _F32)
    den = jnp.zeros((chunk, LANES), _F32)
    for g in range(N_GROUPS_B):
        w = jnp.exp(sm_ref[g] - m_all)
        num = num + w * so_ref[g]
        den = den + w * sl_ref[g]
    o_ref[0] = (num / den).astype(_BF16)


def _t5_buckets(rel):
    half = NUM_BUCKETS // 2
    ret = np.where(rel > 0, half, 0)
    n = np.abs(rel)
    max_exact = half // 2
    large = max_exact + (np.log(np.maximum(n, 1) / max_exact)
                         / np.log(T5_MAX_DIST / max_exact) * (half - max_exact)).astype(np.int32)
    large = np.minimum(large, half - 1)
    return (ret + np.where(n < max_exact, n, large)).astype(np.int32)


def _dilated_bias(t5_bias):
    rel = np.arange(-HALF_KEYS, HALF_KEYS + 1)
    tab = t5_bias.astype(_F32).T
    by_rel = jnp.stack([
        jnp.take(tab[g * HG_B:(g + 1) * HG_B], jnp.asarray(_t5_buckets(rel * dil)), axis=1)
        for g, dil in enumerate(DILATIONS_B)])
    tiles = jnp.stack([
        _toeplitz(by_rel, QBLK_B, KBLK_B, HALF_KEYS - variant * HALF_KEYS)
        for variant in range(3)])
    tiles = tiles.reshape(3, N_GROUPS_B, HG_B // 2, 2 * QBLK_B, KBLK_B)
    return jnp.transpose(tiles, (2, 1, 0, 3, 4))


def _dilated_call(b0, b1, b2, bias):
    B = b0.shape[0]
    T = b0.shape[2]
    chunk = min(CHUNK_B, T)
    assert T % chunk == 0 and chunk % (QBLK_B * max(DILATIONS_B)) == 0
    assert T // max(DILATIONS_B) >= KBLK_B
    in_specs, args = [], []
    for arr, dil in zip((b0, b1, b2), DILATIONS_B):
        sub_len = T // dil
        in_specs += [
            pl.BlockSpec((1, dil, chunk // dil, LANES), lambda b, p, j: (b, 0, j, p)),
            pl.BlockSpec((1, dil, sub_len, LANES), lambda b, p, j: (b, 0, 0, 2 + p)),
            pl.BlockSpec((1, dil, sub_len, LANES), lambda b, p, j: (b, 0, 0, 4 + p)),
        ]
        args += [arr, arr, arr]
    in_specs.append(pl.BlockSpec((1, N_GROUPS_B, 3, 2 * QBLK_B, KBLK_B),
                                 lambda b, p, j: (p, 0, 0, 0, 0)))
    args.append(bias)
    return pl.pallas_call(
        _dilated_kernel,
        grid=(B, HG_B // 2, T // chunk),
        in_specs=in_specs,
        out_specs=pl.BlockSpec((1, chunk, LANES), lambda b, p, j: (b, j, p)),
        out_shape=jax.ShapeDtypeStruct((B, T, GB_W), _BF16),
        scratch_shapes=[pltpu.VMEM((N_GROUPS_B, chunk, LANES), _F32)] * 3,
        compiler_params=pltpu.CompilerParams(
            dimension_semantics=("arbitrary", "arbitrary", "arbitrary"),
            vmem_limit_bytes=VMEM_LIMIT),
        name="dilated",
    )(*args)


def _mlp_kernel(x_ref, oa_ref, ob_ref, gate_ref, wa_ref, wb_ref, wo_ref, gmlp_ref, wup_ref,
                wdn_ref, y_ref):
    ya = jnp.dot(oa_ref[...], wa_ref[...], preferred_element_type=_F32)
    yb = jnp.dot(ob_ref[...], wb_ref[...], preferred_element_type=_F32)
    ga = gate_ref[:, :D_MODEL].astype(_F32)
    gb = gate_ref[:, D_MODEL:].astype(_F32)
    merged = (ga * ya + gb * yb).astype(_BF16)
    x1 = x_ref[...] + jnp.dot(merged, wo_ref[...], preferred_element_type=_F32)
    ms = jnp.mean(x1 * x1, axis=-1, keepdims=True)
    hm = (x1 * lax.rsqrt(ms + EPS) * gmlp_ref[...]).astype(_BF16)
    acc = x1
    for c in range(D_FF // FF_CHUNK):
        sl = slice(c * FF_CHUNK, (c + 1) * FF_CHUNK)
        u = jnp.maximum(jnp.dot(hm, wup_ref[:, sl], preferred_element_type=_F32), 0.0)
        acc = acc + jnp.dot((u * u).astype(_BF16), wdn_ref[sl, :], preferred_element_type=_F32)
    y_ref[...] = acc


def _mlp_call(x2, oa2, ob2, gates2, wa, wb, wo, gmlp, wup, wdn):
    n_tok, D = x2.shape
    tm = TM_MLP
    assert n_tok % tm == 0
    tok = lambda w: pl.BlockSpec((tm, w), lambda i: (i, 0))
    return pl.pallas_call(
        _mlp_kernel,
        grid=(n_tok // tm,),
        in_specs=[tok(D), tok(QA_W), tok(GB_W), tok(2 * D_MODEL),
                  _resident((QA_W, D)), _resident((GB_W, D)), _resident((D, D)),
                  _resident((1, D)), _resident((D, D_FF)), _resident((D_FF, D))],
        out_specs=tok(D),
        out_shape=jax.ShapeDtypeStruct((n_tok, D), _F32),
        compiler_params=pltpu.CompilerParams(
            dimension_semantics=("arbitrary",), vmem_limit_bytes=VMEM_LIMIT),
        name="mlp",
    )(x2, oa2, ob2, gates2, wa, wb, wo, gmlp, wup, wdn)


def _encoder_layer(x, p):
    B, T, D = x.shape
    qa, ka, va, b0, b1, b2, gates = _proj_call(
        x, p["gmix"], p["w_in"], p["gqa"], p["gka"], p["gqb"], p["gkb"], p["bd"])
    oa = _natten_call(qa, ka, va, p["bias_a"])
    ob = _dilated_call(b0, b1, b2, p["bias_b"])
    y = _mlp_call(x.reshape(B * T, D), oa.reshape(B * T, QA_W), ob.reshape(B * T, GB_W),
                  gates.reshape(B * T, 2 * D_MODEL), p["wa"], p["wb"], p["wo"], p["gmlp"],
                  p["wup"], p["wdn"])
    return y.reshape(B, T, D)


def _layer_params(norm_mix, w_in, q_norm_a, k_norm_a, q_norm_b, k_norm_b, rpb_a, t5_bias,
                  w_branch_a, w_branch_b, w_out, norm_mlp, w_up, w_down):
    tile_gain = lambda g: jnp.tile(g.astype(_F32), NORM_CHUNK // HEAD_DIM).reshape(1, NORM_CHUNK)
    head = np.arange(NORM_CHUNK) // HEAD_DIM
    bd = jnp.asarray((head[:, None] == head[None, :]).astype(np.float32) / HEAD_DIM, _BF16)
    return dict(
        gmix=norm_mix.astype(_F32).reshape(1, D_MODEL), w_in=w_in.astype(_BF16),
        gqa=tile_gain(q_norm_a), gka=tile_gain(k_norm_a),
        gqb=tile_gain(q_norm_b), gkb=tile_gain(k_norm_b), bd=bd,
        bias_a=_natten_bias(rpb_a), bias_b=_dilated_bias(t5_bias),
        wa=w_branch_a.astype(_BF16), wb=w_branch_b.astype(_BF16), wo=w_out.astype(_BF16),
        gmlp=norm_mlp.astype(_F32).reshape(1, D_MODEL),
        wup=w_up.astype(_BF16), wdn=w_down.astype(_BF16))


def kernel(x_prompt, x_sample, norm_mix, w_in, q_norm_a, k_norm_a, q_norm_b, k_norm_b, rpb_a,
           t5_bias, w_branch_a, w_branch_b, w_out, norm_mlp, w_up, w_down):
    y_prompt, y_sample = x_prompt, x_sample
    for l in range(norm_mix.shape[0]):
        p = _layer_params(norm_mix[l], w_in[l], q_norm_a[l], k_norm_a[l], q_norm_b[l],
                          k_norm_b[l], rpb_a[l], t5_bias, w_branch_a[l], w_branch_b[l],
                          w_out[l], norm_mlp[l], w_up[l], w_down[l])
        y_prompt = _encoder_layer(y_prompt, p)
        y_sample = _encoder_layer(y_sample, p)
    return (y_prompt, y_sample)
```

```python
import functools
import math

import numpy as np
import jax
import jax.numpy as jnp
from jax import lax
from jax.experimental import pallas as pl
from jax.experimental.pallas import tpu as pltpu

D_MODEL = 1024
HEAD_DIM = 64
H_A = 8
N_GROUPS_B = 3
HG_B = 4
H_B = N_GROUPS_B * HG_B
WINDOWS_B = (128, 512, 2048)
DILATIONS_B = (1, 4, 16)
GRID_W = 64
WIN_ROWS = 8
WIN_COLS = 16
NUM_BUCKETS = 32
T5_MAX_DIST = 1024
D_FF = 4 * D_MODEL
EPS = 1e-6
NEG = -1e30
LOG2E = math.log2(math.e)
QA_W = H_A * HEAD_DIM
QB_W = H_B * HEAD_DIM
GB_W = HG_B * HEAD_DIM
IN_W = 3 * QA_W + 3 * QB_W + 2 * D_MODEL
HALF_KEYS = 64
assert all((w // 2) // d == HALF_KEYS for w, d in zip(WINDOWS_B, DILATIONS_B))

LANES = 128
NORM_CHUNK = 256
TM_PROJ = 512
PROJ_BLOCK = 1024
TM_MLP = 512
FF_CHUNK = 1024
QROWS_A = 4
KROWS_A = QROWS_A + 8
BLOCKS_PER_STEP_A = 16
QBLK_B = 128
KBLK_B = QBLK_B + 2 * HALF_KEYS
CHUNK_B = 2048
TILE_UNROLL_B = 16
VMEM_LIMIT = 56 * 1024 * 1024

_F32 = jnp.float32
_BF16 = jnp.bfloat16


def _resident(shape):
    nd = len(shape)
    return pl.BlockSpec(shape, lambda *_: (0,) * nd, pipeline_mode=pl.Buffered(1))


def _proj_kernel(x_ref, gmix_ref, w_ref, gqa_ref, gka_ref, gqb_ref, gkb_ref,
                 qa_ref, ka_ref, va_ref, b0_ref, b1_ref, b2_ref, gate_ref, scr_ref):
    tm = x_ref.shape[1]
    x = x_ref[0]
    ms = jnp.mean(x * x, axis=-1, keepdims=True)
    h = (x * lax.rsqrt(ms + EPS) * gmix_ref[...]).astype(_BF16)

    wide = {}

    def chunk(c0):
        b0 = (c0 // PROJ_BLOCK) * PROJ_BLOCK
        if b0 not in wide:
            b1 = min(b0 + PROJ_BLOCK, IN_W)
            wide[b0] = jnp.dot(h, w_ref[:, b0:b1], preferred_element_type=_F32)
        return wide[b0][:, c0 - b0:c0 - b0 + NORM_CHUNK]

    lane = lax.broadcasted_iota(jnp.int32, (tm, LANES), 1)
    first = lane < HEAD_DIM

    def head_norm(t, gain_ref, scale):
        parts = []
        for c in range(t.shape[1] // LANES):
            tc = t[:, c * LANES:(c + 1) * LANES]
            sq = tc * tc
            s0 = jnp.sum(jnp.where(first, sq, 0.0), axis=-1, keepdims=True)
            s1 = jnp.sum(jnp.where(first, 0.0, sq), axis=-1, keepdims=True)
            msq = jnp.where(first, s0, s1) * (1.0 / HEAD_DIM)
            parts.append(tc * lax.rsqrt(msq + EPS))
        return jnp.concatenate(parts, axis=1) * (gain_ref[...] * scale)

    qk_scale = HEAD_DIM ** -0.5 * LOG2E

    for c in range(QA_W // NORM_CHUNK):
        sl = slice(c * NORM_CHUNK, (c + 1) * NORM_CHUNK)
        qa_ref[0, :, sl] = head_norm(chunk(sl.start), gqa_ref, qk_scale).astype(_BF16)
        ka_ref[0, :, sl] = head_norm(chunk(QA_W + sl.start), gka_ref, 1.0).astype(_BF16)
        va_ref[0, :, sl] = chunk(2 * QA_W + sl.start).astype(_BF16)

    outs = (b0_ref, b1_ref, b2_ref)
    base = 3 * QA_W
    slab = 0
    for kind in range(3):
        for g, dil in enumerate(DILATIONS_B):
            tg = chunk(base + kind * QB_W + g * GB_W)
            if kind == 0:
                tg = head_norm(tg, gqb_ref, qk_scale)
            elif kind == 1:
                tg = head_norm(tg, gkb_ref, 1.0)
            lane0 = kind * GB_W
            if dil == 1:
                outs[g][0, 0, :, lane0:lane0 + GB_W] = tg.astype(_BF16)
                continue
            for half in range(GB_W // LANES):
                scr_ref[slab] = tg[:, half * LANES:(half + 1) * LANES]
                for r in range(dil):
                    sub = scr_ref[slab, pl.ds(r, tm // dil, stride=dil), :]
                    l0 = lane0 + half * LANES
                    outs[g][0, r, :, l0:l0 + LANES] = sub.astype(_BF16)
                slab += 1

    base = 3 * QA_W + 3 * QB_W
    for c in range(2 * D_MODEL // NORM_CHUNK):
        sl = slice(c * NORM_CHUNK, (c + 1) * NORM_CHUNK)
        gate_ref[0, :, sl] = jax.nn.sigmoid(chunk(base + sl.start)).astype(_BF16)


def _proj_call(x, gmix, w_in, gqa, gka, gqb, gkb):
    B, T, D = x.shape
    tm = TM_PROJ
    assert T % tm == 0 and tm % (16 * max(DILATIONS_B)) == 0
    n_slabs = 3 * sum(1 for d in DILATIONS_B if d > 1) * (GB_W // LANES)
    tok = lambda w: pl.BlockSpec((1, tm, w), lambda b, i: (b, i, 0))
    sub = lambda d: pl.BlockSpec((1, d, tm // d, 3 * GB_W), lambda b, i: (b, 0, i, 0))
    out_shape = (
        jax.ShapeDtypeStruct((B, T, QA_W), _BF16),
        jax.ShapeDtypeStruct((B, T, QA_W), _BF16),
        jax.ShapeDtypeStruct((B, T, QA_W), _BF16),
    ) + tuple(jax.ShapeDtypeStruct((B, d, T // d, 3 * GB_W), _BF16) for d in DILATIONS_B) + (
        jax.ShapeDtypeStruct((B, T, 2 * D_MODEL), _BF16),
    )
    return pl.pallas_call(
        _proj_kernel,
        grid=(B, T // tm),
        in_specs=[tok(D), _resident((1, D)), _resident((D, IN_W)),
                  _resident((1, NORM_CHUNK)), _resident((1, NORM_CHUNK)),
                  _resident((1, NORM_CHUNK)), _resident((1, NORM_CHUNK))],
        out_specs=(tok(QA_W), tok(QA_W), tok(QA_W)) + tuple(sub(d) for d in DILATIONS_B)
        + (tok(2 * D_MODEL),),
        out_shape=out_shape,
        scratch_shapes=[pltpu.VMEM((n_slabs, tm, LANES), _F32)],
        compiler_params=pltpu.CompilerParams(
            dimension_semantics=("arbitrary", "arbitrary"), vmem_limit_bytes=VMEM_LIMIT),
        name="proj",
    )(x, gmix, w_in, gqa, gka, gqb, gkb)


def _pair_attention(q, kw, vw, bias2):
    m_rows = q.shape[0]
    lane = lax.broadcasted_iota(jnp.int32, (m_rows, LANES), 1)
    first = lane < HEAD_DIM
    zero = jnp.zeros_like(q)
    q2 = jnp.concatenate([jnp.where(first, q, zero), jnp.where(first, zero, q)], axis=0)
    s = lax.dot_general(q2, kw, (((1,), (1,)), ((), ())), preferred_element_type=_F32)
    s = s + bias2
    m = jnp.max(s, axis=-1, keepdims=True)
    p = jnp.exp2(s - m)
    l = jnp.sum(p, axis=-1, keepdims=True)
    o2 = jnp.dot(p.astype(_BF16), vw, preferred_element_type=_F32)

    def pick(x2):
        x2 = jnp.broadcast_to(x2, (2 * m_rows, LANES))
        return jnp.where(first, x2[:m_rows], x2[m_rows:])

    return pick(o2), pick(m), pick(l)


def _toeplitz(v, n_rows, n_cols, center):
    n = v.shape[-1]
    period = n_rows + n_cols - 1
    left = n_rows - 1 - center
    cfg = [(0, 0, 0)] * (v.ndim - 1) + [(left, period - n - left, 0)]
    ext = lax.pad(v, jnp.asarray(NEG, v.dtype), cfg)
    w = jnp.concatenate([ext[..., n_rows - 1:], ext[..., :n_rows - 1]], axis=-1)
    flat = jnp.tile(w, (1,) * (v.ndim - 1) + (n_rows,))[..., :n_rows * (period - 1)]
    return flat.reshape(v.shape[:-1] + (n_rows, period - 1))[..., :n_cols]


def _natten_kernel(q_ref, k_ref, v_ref, bias_ref, o_ref, *, rows):
    i = pl.program_id(2)
    mq, nk = QROWS_A * GRID_W, KROWS_A * GRID_W
    last_blk = rows // QROWS_A - 1
    for u in range(BLOCKS_PER_STEP_A):
        blk = BLOCKS_PER_STEP_A * i + u
        krow0 = jnp.clip(QROWS_A * blk - WIN_ROWS // 2, 0, rows - KROWS_A)
        k0 = pl.multiple_of(krow0 * GRID_W, GRID_W)
        variant = jnp.where(blk == 0, 0, jnp.where(blk == last_blk, 2, 1))
        kw = k_ref[0, pl.ds(k0, nk), :]
        vw = v_ref[0, pl.ds(k0, nk), :]
        q = q_ref[0, u * mq:(u + 1) * mq, :]
        o, _, l = _pair_attention(q, kw, vw, bias_ref[variant, 0])
        o_ref[0, u * mq:(u + 1) * mq, :] = (o / l).astype(_BF16)


def _natten_bias(rpb):
    n_off = 2 * WIN_ROWS - 1
    toep = _toeplitz(rpb.astype(_F32) * LOG2E, GRID_W, GRID_W, WIN_COLS - 1)
    c = np.arange(GRID_W)[:, None]
    kc = np.arange(GRID_W)[None, :]
    col_start = np.clip(c - WIN_COLS // 2, 0, GRID_W - WIN_COLS)
    col_ok = (kc >= col_start) & (kc < col_start + WIN_COLS)
    col_bias = jnp.where(jnp.asarray(col_ok), toep, NEG)
    neg_block = jnp.full((H_A, GRID_W, GRID_W), NEG, _F32)
    fake_rows = 3 * KROWS_A
    nblk = fake_rows // QROWS_A
    variants = []
    for blk in (0, nblk // 2, nblk - 1):
        krow0 = int(np.clip(QROWS_A * blk - WIN_ROWS // 2, 0, fake_rows - KROWS_A))
        q_rows = []
        for rq in range(QROWS_A):
            r = QROWS_A * blk + rq
            start = int(np.clip(r - WIN_ROWS // 2, 0, fake_rows - WIN_ROWS))
            blocks = []
            for rk in range(KROWS_A):
                kr = krow0 + rk
                row_off = kr - r + WIN_ROWS - 1
                inside = start <= kr < start + WIN_ROWS
                assert not inside or 0 <= row_off < n_off
                blocks.append(col_bias[:, row_off] if inside else neg_block)
            q_rows.append(jnp.concatenate(blocks, axis=-1))
        variants.append(jnp.concatenate(q_rows, axis=-2))
    tiles = jnp.stack(variants)
    return tiles.reshape(3, H_A // 2, 2 * QROWS_A * GRID_W, KROWS_A * GRID_W)


def _natten_call(qa, ka, va, bias):
    B, T, _ = qa.shape
    rows = T // GRID_W
    rows_per_step = QROWS_A * BLOCKS_PER_STEP_A
    assert T % GRID_W == 0 and rows % rows_per_step == 0 and rows >= KROWS_A
    mq, nk = rows_per_step * GRID_W, KROWS_A * GRID_W
    return pl.pallas_call(
        functools.partial(_natten_kernel, rows=rows),
        grid=(B, H_A // 2, rows // rows_per_step),
        in_specs=[pl.BlockSpec((1, mq, LANES), lambda b, p, i: (b, i, p)),
                  pl.BlockSpec((1, T, LANES), lambda b, p, i: (b, 0, p)),
                  pl.BlockSpec((1, T, LANES), lambda b, p, i: (b, 0, p)),
                  pl.BlockSpec((3, 1, 2 * QROWS_A * GRID_W, nk), lambda b, p, i: (0, p, 0, 0))],
        out_specs=pl.BlockSpec((1, mq, LANES), lambda b, p, i: (b, i, p)),
        out_shape=jax.ShapeDtypeStruct((B, T, QA_W), _BF16),
        compiler_params=pltpu.CompilerParams(
            dimension_semantics=("arbitrary", "arbitrary", "arbitrary"),
            vmem_limit_bytes=VMEM_LIMIT),
        name="natten",
    )(qa, ka, va, bias)


def _dilated_kernel(q0_ref, k0_ref, v0_ref, q1_ref, k1_ref, v1_ref, q2_ref, k2_ref, v2_ref,
                    bias_ref, o_ref, so_ref, sm_ref, sl_ref):
    j = pl.program_id(2)
    chunk = o_ref.shape[1]
    refs = ((q0_ref, k0_ref, v0_ref), (q1_ref, k1_ref, v1_ref), (q2_ref, k2_ref, v2_ref))
    for g, dil in enumerate(DILATIONS_B):
        q_ref, k_ref, v_ref = refs[g]
        sub_len = k_ref.shape[2]
        per_res = chunk // dil
        nsb = per_res // QBLK_B

        def tile(n, carry, g=g, dil=dil, q_ref=q_ref, k_ref=k_ref, v_ref=v_ref,
                 sub_len=sub_len, per_res=per_res, nsb=nsb):
            r = n // nsb
            sb = n % nsb
            lq0 = j * per_res + sb * QBLK_B
            kl0 = jnp.clip(lq0 - HALF_KEYS, 0, sub_len - KBLK_B)
            variant = (lq0 - kl0) // HALF_KEYS
            kl0 = pl.multiple_of(kl0, HALF_KEYS)
            q = q_ref[0, r, pl.ds(pl.multiple_of(sb * QBLK_B, QBLK_B), QBLK_B), :]
            kw = k_ref[0, r, pl.ds(kl0, KBLK_B), :]
            vw = v_ref[0, r, pl.ds(kl0, KBLK_B), :]
            o, m, l = _pair_attention(q, kw, vw, bias_ref[0, g, variant])
            if dil == 1:
                rows_out = pl.ds(pl.multiple_of(sb * QBLK_B, QBLK_B), QBLK_B)
            else:
                rows_out = pl.ds(sb * QBLK_B * dil + r, QBLK_B, stride=dil)
            so_ref[g, rows_out, :] = o
            sm_ref[g, rows_out, :] = m
            sl_ref[g, rows_out, :] = l
            return carry

        lax.fori_loop(0, dil * nsb, tile, 0, unroll=TILE_UNROLL_B)

    m_all = jnp.maximum(jnp.maximum(sm_ref[0], sm_ref[1]), sm_ref[2])
    num = jnp.zeros((chunk, LANES), _F32)
    den = jnp.zeros((chunk, LANES), _F32)
    for g in range(N_GROUPS_B):
        w = jnp.exp2(sm_ref[g] - m_all)
        num = num + w * so_ref[g]
        den = den + w * sl_ref[g]
    o_ref[0] = (num / den).astype(_BF16)


def _t5_buckets(rel):
    half = NUM_BUCKETS // 2
    ret = np.where(rel > 0, half, 0)
    n = np.abs(rel)
    max_exact = half // 2
    large = max_exact + (np.log(np.maximum(n, 1) / max_exact)
                         / np.log(T5_MAX_DIST / max_exact) * (half - max_exact)).astype(np.int32)
    large = np.minimum(large, half - 1)
    return (ret + np.where(n < max_exact, n, large)).astype(np.int32)


def _dilated_bias(t5_bias):
    rel = np.arange(-HALF_KEYS, HALF_KEYS + 1)
    tab = t5_bias.astype(_F32).T * LOG2E
    by_rel = jnp.stack([
        jnp.take(tab[g * HG_B:(g + 1) * HG_B], jnp.asarray(_t5_buckets(rel * dil)), axis=1)
        for g, dil in enumerate(DILATIONS_B)])
    tiles = jnp.stack([
        _toeplitz(by_rel, QBLK_B, KBLK_B, HALF_KEYS - variant * HALF_KEYS)
        for variant in range(3)])
    tiles = tiles.reshape(3, N_GROUPS_B, HG_B // 2, 2 * QBLK_B, KBLK_B)
    return jnp.transpose(tiles, (2, 1, 0, 3, 4))


def _dilated_call(b0, b1, b2, bias):
    B = b0.shape[0]
    T = b0.shape[2]
    chunk = min(CHUNK_B, T)
    assert T % chunk == 0 and chunk % (QBLK_B * max(DILATIONS_B)) == 0
    assert T // max(DILATIONS_B) >= KBLK_B
    in_specs, args = [], []
    for arr, dil in zip((b0, b1, b2), DILATIONS_B):
        sub_len = T // dil
        in_specs += [
            pl.BlockSpec((1, dil, chunk // dil, LANES), lambda b, p, j: (b, 0, j, p)),
            pl.BlockSpec((1, dil, sub_len, LANES), lambda b, p, j: (b, 0, 0, 2 + p)),
            pl.BlockSpec((1, dil, sub_len, LANES), lambda b, p, j: (b, 0, 0, 4 + p)),
        ]
        args += [arr, arr, arr]
    in_specs.append(pl.BlockSpec((1, N_GROUPS_B, 3, 2 * QBLK_B, KBLK_B),
                                 lambda b, p, j: (p, 0, 0, 0, 0)))
    args.append(bias)
    return pl.pallas_call(
        _dilated_kernel,
        grid=(B, HG_B // 2, T // chunk),
        in_specs=in_specs,
        out_specs=pl.BlockSpec((1, chunk, LANES), lambda b, p, j: (b, j, p)),
        out_shape=jax.ShapeDtypeStruct((B, T, GB_W), _BF16),
        scratch_shapes=[pltpu.VMEM((N_GROUPS_B, chunk, LANES), _F32)] * 3,
        compiler_params=pltpu.CompilerParams(
            dimension_semantics=("arbitrary", "arbitrary", "arbitrary"),
            vmem_limit_bytes=VMEM_LIMIT),
        name="dilated",
    )(*args)


def _mlp_kernel(x_ref, oa_ref, ob_ref, gate_ref, wa_ref, wb_ref, wo_ref, gmlp_ref, wup_ref,
                wdn_ref, y_ref):
    ya = jnp.dot(oa_ref[...], wa_ref[...], preferred_element_type=_F32)
    yb = jnp.dot(ob_ref[...], wb_ref[...], preferred_element_type=_F32)
    ga = gate_ref[:, :D_MODEL].astype(_F32)
    gb = gate_ref[:, D_MODEL:].astype(_F32)
    merged = (ga * ya + gb * yb).astype(_BF16)
    x1 = x_ref[...] + jnp.dot(merged, wo_ref[...], preferred_element_type=_F32)
    ms = jnp.mean(x1 * x1, axis=-1, keepdims=True)
    hm = (x1 * lax.rsqrt(ms + EPS) * gmlp_ref[...]).astype(_BF16)
    acc = x1
    for c in range(D_FF // FF_CHUNK):
        sl = slice(c * FF_CHUNK, (c + 1) * FF_CHUNK)
        u = jnp.maximum(jnp.dot(hm, wup_ref[:, sl], preferred_element_type=_F32), 0.0)
        acc = acc + jnp.dot((u * u).astype(_BF16), wdn_ref[sl, :], preferred_element_type=_F32)
    y_ref[...] = acc


def _mlp_call(x2, oa2, ob2, gates2, wa, wb, wo, gmlp, wup, wdn):
    n_tok, D = x2.shape
    tm = TM_MLP
    assert n_tok % tm == 0
    tok = lambda w: pl.BlockSpec((tm, w), lambda i: (i, 0))
    return pl.pallas_call(
        _mlp_kernel,
        grid=(n_tok // tm,),
        in_specs=[tok(D), tok(QA_W), tok(GB_W), tok(2 * D_MODEL),
                  _resident((QA_W, D)), _resident((GB_W, D)), _resident((D, D)),
                  _resident((1, D)), _resident((D, D_FF)), _resident((D_FF, D))],
        out_specs=tok(D),
        out_shape=jax.ShapeDtypeStruct((n_tok, D), _F32),
        compiler_params=pltpu.CompilerParams(
            dimension_semantics=("arbitrary",), vmem_limit_bytes=VMEM_LIMIT),
        name="mlp",
    )(x2, oa2, ob2, gates2, wa, wb, wo, gmlp, wup, wdn)


def _encoder_layer(x, p):
    B, T, D = x.shape
    qa, ka, va, b0, b1, b2, gates = _proj_call(
        x, p["gmix"], p["w_in"], p["gqa"], p["gka"], p["gqb"], p["gkb"])
    oa = _natten_call(qa, ka, va, p["bias_a"])
    ob = _dilated_call(b0, b1, b2, p["bias_b"])
    y = _mlp_call(x.reshape(B * T, D), oa.reshape(B * T, QA_W), ob.reshape(B * T, GB_W),
                  gates.reshape(B * T, 2 * D_MODEL), p["wa"], p["wb"], p["wo"], p["gmlp"],
                  p["wup"], p["wdn"])
    return y.reshape(B, T, D)


def _layer_params(norm_mix, w_in, q_norm_a, k_norm_a, q_norm_b, k_norm_b, rpb_a, t5_bias,
                  w_branch_a, w_branch_b, w_out, norm_mlp, w_up, w_down):
    tile_gain = lambda g: jnp.tile(g.astype(_F32), NORM_CHUNK // HEAD_DIM).reshape(1, NORM_CHUNK)
    return dict(
        gmix=norm_mix.astype(_F32).reshape(1, D_MODEL), w_in=w_in.astype(_BF16),
        gqa=tile_gain(q_norm_a), gka=tile_gain(k_norm_a),
        gqb=tile_gain(q_norm_b), gkb=tile_gain(k_norm_b),
        bias_a=_natten_bias(rpb_a), bias_b=_dilated_bias(t5_bias),
        wa=w_branch_a.astype(_BF16), wb=w_branch_b.astype(_BF16), wo=w_out.astype(_BF16),
        gmlp=norm_mlp.astype(_F32).reshape(1, D_MODEL),
        wup=w_up.astype(_BF16), wdn=w_down.astype(_BF16))


def kernel(x_prompt, x_sample, norm_mix, w_in, q_norm_a, k_norm_a, q_norm_b, k_norm_b, rpb_a,
           t5_bias, w_branch_a, w_branch_b, w_out, norm_mlp, w_up, w_down):
    y_prompt, y_sample = x_prompt, x_sample
    for l in range(norm_mix.shape[0]):
        p = _layer_params(norm_mix[l], w_in[l], q_norm_a[l], k_norm_a[l], q_norm_b[l],
                          k_norm_b[l], rpb_a[l], t5_bias, w_branch_a[l], w_branch_b[l],
                          w_out[l], norm_mlp[l], w_up[l], w_down[l])
        y_prompt = _encoder_layer(y_prompt, p)
        y_sample = _encoder_layer(y_sample, p)
    return (y_prompt, y_sample)
```

```python
import functools
import math

import numpy as np
import jax
import jax.numpy as jnp
from jax import lax
from jax.experimental import pallas as pl
from jax.experimental.pallas import tpu as pltpu

D_MODEL = 1024
HEAD_DIM = 64
H_A = 8
N_GROUPS_B = 3
HG_B = 4
H_B = N_GROUPS_B * HG_B
WINDOWS_B = (128, 512, 2048)
DILATIONS_B = (1, 4, 16)
GRID_W = 64
WIN_ROWS = 8
WIN_COLS = 16
NUM_BUCKETS = 32
T5_MAX_DIST = 1024
D_FF = 4 * D_MODEL
EPS = 1e-6
NEG = -1e30
LOG2E = math.log2(math.e)
QA_W = H_A * HEAD_DIM
QB_W = H_B * HEAD_DIM
GB_W = HG_B * HEAD_DIM
IN_W = 3 * QA_W + 3 * QB_W + 2 * D_MODEL
HALF_KEYS = 64
assert all((w // 2) // d == HALF_KEYS for w, d in zip(WINDOWS_B, DILATIONS_B))

LANES = 128
NORM_CHUNK = 256
TM_PROJ = 512
DEINT_STEP = 4
PROJ_BLOCK = 1024
TM_MLP = 512
FF_CHUNK = 1024
QROWS_A = 4
KROWS_A = QROWS_A + 8
BLOCKS_PER_STEP_A = 16
QBLK_B = 128
KBLK_B = QBLK_B + 2 * HALF_KEYS
CHUNK_B = 2048
VMEM_LIMIT = 56 * 1024 * 1024

_F32 = jnp.float32
_BF16 = jnp.bfloat16


def _resident(shape):
    nd = len(shape)
    return pl.BlockSpec(shape, lambda *_: (0,) * nd, pipeline_mode=pl.Buffered(1))


def _proj_kernel(x_ref, gmix_ref, w_ref, gqa_ref, gka_ref, gqb_ref, gkb_ref,
                 qa_ref, ka_ref, va_ref, b0_ref, b1_ref, b2_ref, gate_ref, scr_ref, scr2_ref):
    tm = x_ref.shape[1]
    x = x_ref[0]
    ms = jnp.mean(x * x, axis=-1, keepdims=True)
    h = (x * lax.rsqrt(ms + EPS) * gmix_ref[...]).astype(_BF16)

    wide = {}

    def chunk(c0):
        b0 = (c0 // PROJ_BLOCK) * PROJ_BLOCK
        if b0 not in wide:
            b1 = min(b0 + PROJ_BLOCK, IN_W)
            wide[b0] = jnp.dot(h, w_ref[:, b0:b1], preferred_element_type=_F32)
        return wide[b0][:, c0 - b0:c0 - b0 + NORM_CHUNK]

    lane = lax.broadcasted_iota(jnp.int32, (tm, LANES), 1)
    first = lane < HEAD_DIM

    def head_norm(t, gain_ref, scale):
        parts = []
        for c in range(t.shape[1] // LANES):
            tc = t[:, c * LANES:(c + 1) * LANES]
            sq = tc * tc
            s0 = jnp.sum(jnp.where(first, sq, 0.0), axis=-1, keepdims=True)
            s1 = jnp.sum(jnp.where(first, 0.0, sq), axis=-1, keepdims=True)
            msq = jnp.where(first, s0, s1) * (1.0 / HEAD_DIM)
            parts.append(tc * lax.rsqrt(msq + EPS))
        return jnp.concatenate(parts, axis=1) * (gain_ref[...] * scale)

    qk_scale = HEAD_DIM ** -0.5 * LOG2E

    for c in range(QA_W // NORM_CHUNK):
        sl = slice(c * NORM_CHUNK, (c + 1) * NORM_CHUNK)
        qa_ref[0, :, sl] = head_norm(chunk(sl.start), gqa_ref, qk_scale).astype(_BF16)
        ka_ref[0, :, sl] = head_norm(chunk(QA_W + sl.start), gka_ref, 1.0).astype(_BF16)
        va_ref[0, :, sl] = chunk(2 * QA_W + sl.start).astype(_BF16)

    outs = (b0_ref, b1_ref, b2_ref)
    base = 3 * QA_W
    slab = slab2 = 0
    for kind in range(3):
        for g, dil in enumerate(DILATIONS_B):
            tg = chunk(base + kind * QB_W + g * GB_W)
            if kind == 0:
                tg = head_norm(tg, gqb_ref, qk_scale)
            elif kind == 1:
                tg = head_norm(tg, gkb_ref, 1.0)
            lane0 = kind * GB_W
            if dil == 1:
                outs[g][0, 0, :, lane0:lane0 + GB_W] = tg.astype(_BF16)
                continue
            for half in range(GB_W // LANES):
                l0 = lane0 + half * LANES
                scr_ref[slab] = tg[:, half * LANES:(half + 1) * LANES]
                if dil == DEINT_STEP:
                    for r in range(dil):
                        sub = scr_ref[slab, pl.ds(r, tm // dil, stride=dil), :]
                        outs[g][0, r, :, l0:l0 + LANES] = sub.astype(_BF16)
                else:
                    assert dil == DEINT_STEP * DEINT_STEP
                    for lo in range(DEINT_STEP):
                        scr2_ref[slab2, lo] = scr_ref[
                            slab, pl.ds(lo, tm // DEINT_STEP, stride=DEINT_STEP), :]
                    for lo in range(DEINT_STEP):
                        for hi in range(DEINT_STEP):
                            sub = scr2_ref[slab2, lo, pl.ds(hi, tm // dil, stride=DEINT_STEP), :]
                            outs[g][0, hi * DEINT_STEP + lo, :, l0:l0 + LANES] = sub.astype(_BF16)
                    slab2 += 1
                slab += 1

    base = 3 * QA_W + 3 * QB_W
    for c in range(2 * D_MODEL // NORM_CHUNK):
        sl = slice(c * NORM_CHUNK, (c + 1) * NORM_CHUNK)
        gate_ref[0, :, sl] = jax.nn.sigmoid(chunk(base + sl.start)).astype(_BF16)


def _proj_call(x, gmix, w_in, gqa, gka, gqb, gkb):
    B, T, D = x.shape
    tm = TM_PROJ
    assert T % tm == 0 and tm % (16 * max(DILATIONS_B)) == 0
    n_slabs = 3 * sum(1 for d in DILATIONS_B if d > 1) * (GB_W // LANES)
    n_slabs2 = 3 * sum(1 for d in DILATIONS_B if d > DEINT_STEP) * (GB_W // LANES)
    tok = lambda w: pl.BlockSpec((1, tm, w), lambda b, i: (b, i, 0))
    sub = lambda d: pl.BlockSpec((1, d, tm // d, 3 * GB_W), lambda b, i: (b, 0, i, 0))
    out_shape = (
        jax.ShapeDtypeStruct((B, T, QA_W), _BF16),
        jax.ShapeDtypeStruct((B, T, QA_W), _BF16),
        jax.ShapeDtypeStruct((B, T, QA_W), _BF16),
    ) + tuple(jax.ShapeDtypeStruct((B, d, T // d, 3 * GB_W), _BF16) for d in DILATIONS_B) + (
        jax.ShapeDtypeStruct((B, T, 2 * D_MODEL), _BF16),
    )
    return pl.pallas_call(
        _proj_kernel,
        grid=(B, T // tm),
        in_specs=[tok(D), _resident((1, D)), _resident((D, IN_W)),
                  _resident((1, NORM_CHUNK)), _resident((1, NORM_CHUNK)),
                  _resident((1, NORM_CHUNK)), _resident((1, NORM_CHUNK))],
        out_specs=(tok(QA_W), tok(QA_W), tok(QA_W)) + tuple(sub(d) for d in DILATIONS_B)
        + (tok(2 * D_MODEL),),
        out_shape=out_shape,
        scratch_shapes=[pltpu.VMEM((n_slabs, tm, LANES), _F32),
                        pltpu.VMEM((n_slabs2, DEINT_STEP, tm // DEINT_STEP, LANES), _F32)],
        compiler_params=pltpu.CompilerParams(
            dimension_semantics=("arbitrary", "arbitrary"), vmem_limit_bytes=VMEM_LIMIT),
        name="proj",
    )(x, gmix, w_in, gqa, gka, gqb, gkb)


def _pair_attention(q, kw, vw, bias2):
    m_rows = q.shape[0]
    lane = lax.broadcasted_iota(jnp.int32, (m_rows, LANES), 1)
    first = lane < HEAD_DIM
    zero = jnp.zeros_like(q)
    q2 = jnp.concatenate([jnp.where(first, q, zero), jnp.where(first, zero, q)], axis=0)
    s = lax.dot_general(q2, kw, (((1,), (1,)), ((), ())), preferred_element_type=_F32)
    s = s + bias2
    m = jnp.max(s, axis=-1, keepdims=True)
    p = jnp.exp2(s - m)
    l = jnp.sum(p, axis=-1, keepdims=True)
    o2 = jnp.dot(p.astype(_BF16), vw, preferred_element_type=_F32)

    def pick(x2):
        x2 = jnp.broadcast_to(x2, (2 * m_rows, LANES))
        return jnp.where(first, x2[:m_rows], x2[m_rows:])

    return pick(o2), pick(m), pick(l)


def _toeplitz(v, n_rows, n_cols, center):
    n = v.shape[-1]
    period = n_rows + n_cols - 1
    left = n_rows - 1 - center
    cfg = [(0, 0, 0)] * (v.ndim - 1) + [(left, period - n - left, 0)]
    ext = lax.pad(v, jnp.asarray(NEG, v.dtype), cfg)
    w = jnp.concatenate([ext[..., n_rows - 1:], ext[..., :n_rows - 1]], axis=-1)
    flat = jnp.tile(w, (1,) * (v.ndim - 1) + (n_rows,))[..., :n_rows * (period - 1)]
    return flat.reshape(v.shape[:-1] + (n_rows, period - 1))[..., :n_cols]


def _natten_kernel(q_ref, k_ref, v_ref, bias_ref, o_ref, *, rows):
    i = pl.program_id(2)
    mq, nk = QROWS_A * GRID_W, KROWS_A * GRID_W
    last_blk = rows // QROWS_A - 1
    blocks_per_step = q_ref.shape[1] // mq
    for u in range(blocks_per_step):
        blk = blocks_per_step * i + u
        krow0 = jnp.clip(QROWS_A * blk - WIN_ROWS // 2, 0, rows - KROWS_A)
        k0 = pl.multiple_of(krow0 * GRID_W, GRID_W)
        variant = jnp.where(blk == 0, 0, jnp.where(blk == last_blk, 2, 1))
        kw = k_ref[0, pl.ds(k0, nk), :]
        vw = v_ref[0, pl.ds(k0, nk), :]
        q = q_ref[0, u * mq:(u + 1) * mq, :]
        o, _, l = _pair_attention(q, kw, vw, bias_ref[variant, 0])
        o_ref[0, u * mq:(u + 1) * mq, :] = (o / l).astype(_BF16)


def _natten_bias(rpb):
    n_off = 2 * WIN_ROWS - 1
    toep = _toeplitz(rpb.astype(_F32) * LOG2E, GRID_W, GRID_W, WIN_COLS - 1)
    c = np.arange(GRID_W)[:, None]
    kc = np.arange(GRID_W)[None, :]
    col_start = np.clip(c - WIN_COLS // 2, 0, GRID_W - WIN_COLS)
    col_ok = (kc >= col_start) & (kc < col_start + WIN_COLS)
    col_bias = jnp.where(jnp.asarray(col_ok), toep, NEG)
    neg_block = jnp.full((H_A, GRID_W, GRID_W), NEG, _F32)
    fake_rows = 3 * KROWS_A
    nblk = fake_rows // QROWS_A
    variants = []
    for blk in (0, nblk // 2, nblk - 1):
        krow0 = int(np.clip(QROWS_A * blk - WIN_ROWS // 2, 0, fake_rows - KROWS_A))
        q_rows = []
        for rq in range(QROWS_A):
            r = QROWS_A * blk + rq
            start = int(np.clip(r - WIN_ROWS // 2, 0, fake_rows - WIN_ROWS))
            blocks = []
            for rk in range(KROWS_A):
                kr = krow0 + rk
                row_off = kr - r + WIN_ROWS - 1
                inside = start <= kr < start + WIN_ROWS
                assert not inside or 0 <= row_off < n_off
                blocks.append(col_bias[:, row_off] if inside else neg_block)
            q_rows.append(jnp.concatenate(blocks, axis=-1))
        variants.append(jnp.concatenate(q_rows, axis=-2))
    tiles = jnp.stack(variants)
    return tiles.reshape(3, H_A // 2, 2 * QROWS_A * GRID_W, KROWS_A * GRID_W)


def _natten_call(qa, ka, va, bias):
    B, T, _ = qa.shape
    rows = T // GRID_W
    rows_per_step = QROWS_A * min(BLOCKS_PER_STEP_A, rows // QROWS_A)
    assert T % GRID_W == 0 and rows % rows_per_step == 0 and rows >= KROWS_A
    mq, nk = rows_per_step * GRID_W, KROWS_A * GRID_W
    return pl.pallas_call(
        functools.partial(_natten_kernel, rows=rows),
        grid=(B, H_A // 2, rows // rows_per_step),
        in_specs=[pl.BlockSpec((1, mq, LANES), lambda b, p, i: (b, i, p)),
                  pl.BlockSpec((1, T, LANES), lambda b, p, i: (b, 0, p)),
                  pl.BlockSpec((1, T, LANES), lambda b, p, i: (b, 0, p)),
                  pl.BlockSpec((3, 1, 2 * QROWS_A * GRID_W, nk), lambda b, p, i: (0, p, 0, 0))],
        out_specs=pl.BlockSpec((1, mq, LANES), lambda b, p, i: (b, i, p)),
        out_shape=jax.ShapeDtypeStruct((B, T, QA_W), _BF16),
        compiler_params=pltpu.CompilerParams(
            dimension_semantics=("arbitrary", "arbitrary", "arbitrary"),
            vmem_limit_bytes=VMEM_LIMIT),
        name="natten",
    )(qa, ka, va, bias)


def _dilated_kernel(q0_ref, k0_ref, v0_ref, q1_ref, k1_ref, v1_ref, q2_ref, k2_ref, v2_ref,
                    bias_ref, o_ref, so_ref, sm_ref, sl_ref, stage_ref):
    j = pl.program_id(2)
    chunk = o_ref.shape[1]
    refs = ((q0_ref, k0_ref, v0_ref), (q1_ref, k1_ref, v1_ref), (q2_ref, k2_ref, v2_ref))

    def tile(g, r, sb):
        dil = DILATIONS_B[g]
        q_ref, k_ref, v_ref = refs[g]
        sub_len = k_ref.shape[2]
        lq0 = j * (chunk // dil) + sb * QBLK_B
        kl0 = jnp.clip(lq0 - HALF_KEYS, 0, sub_len - KBLK_B)
        variant = (lq0 - kl0) // HALF_KEYS
        kl0 = pl.multiple_of(kl0, HALF_KEYS)
        q = q_ref[0, r, sb * QBLK_B:(sb + 1) * QBLK_B, :]
        kw = k_ref[0, r, pl.ds(kl0, KBLK_B), :]
        vw = v_ref[0, r, pl.ds(kl0, KBLK_B), :]
        res = _pair_attention(q, kw, vw, bias_ref[0, g, variant])
        if dil > DEINT_STEP:
            lo, hi = r % DEINT_STEP, r // DEINT_STEP
            rows_out = pl.ds(sb * QBLK_B * DEINT_STEP + hi, QBLK_B, stride=DEINT_STEP)
            for a, val in enumerate(res):
                stage_ref[a, lo, rows_out, :] = val
            return
        if dil == 1:
            rows_out = pl.ds(sb * QBLK_B, QBLK_B)
        else:
            rows_out = pl.ds(sb * QBLK_B * dil + r, QBLK_B, stride=dil)
        for dst, val in zip((so_ref, sm_ref, sl_ref), res):
            dst[g, rows_out, :] = val

    def flush_stage(g, lo):
        rows_out = pl.ds(lo, chunk // DEINT_STEP, stride=DEINT_STEP)
        for a, dst in enumerate((so_ref, sm_ref, sl_ref)):
            dst[g, rows_out, :] = stage_ref[a, lo]

    def combine(rows):
        m_all = jnp.maximum(jnp.maximum(sm_ref[0, rows, :], sm_ref[1, rows, :]), sm_ref[2, rows, :])
        num = den = None
        for g in range(N_GROUPS_B):
            w = jnp.exp2(sm_ref[g, rows, :] - m_all)
            num = w * so_ref[g, rows, :] if num is None else num + w * so_ref[g, rows, :]
            den = w * sl_ref[g, rows, :] if den is None else den + w * sl_ref[g, rows, :]
        o_ref[0, rows, :] = (num / den).astype(_BF16)

    for g in reversed(range(N_GROUPS_B)):
        dil = DILATIONS_B[g]
        nsb = chunk // dil // QBLK_B
        if dil > DEINT_STEP:
            assert dil == DEINT_STEP * DEINT_STEP
            for lo in range(DEINT_STEP):
                for hi in range(DEINT_STEP):
                    for sb in range(nsb):
                        tile(g, hi * DEINT_STEP + lo, sb)
                flush_stage(g, lo)
            continue
        for r in range(dil):
            for sb in range(nsb):
                tile(g, r, sb)
                if dil == 1:
                    combine(slice(sb * QBLK_B, (sb + 1) * QBLK_B))


def _t5_buckets(rel):
    half = NUM_BUCKETS // 2
    ret = np.where(rel > 0, half, 0)
    n = np.abs(rel)
    max_exact = half // 2
    large = max_exact + (np.log(np.maximum(n, 1) / max_exact)
                         / np.log(T5_MAX_DIST / max_exact) * (half - max_exact)).astype(np.int32)
    large = np.minimum(large, half - 1)
    return (ret + np.where(n < max_exact, n, large)).astype(np.int32)


def _dilated_bias(t5_bias):
    rel = np.arange(-HALF_KEYS, HALF_KEYS + 1)
    tab = t5_bias.astype(_F32).T * LOG2E
    by_rel = jnp.stack([
        jnp.take(tab[g * HG_B:(g + 1) * HG_B], jnp.asarray(_t5_buckets(rel * dil)), axis=1)
        for g, dil in enumerate(DILATIONS_B)])
    tiles = jnp.stack([
        _toeplitz(by_rel, QBLK_B, KBLK_B, HALF_KEYS - variant * HALF_KEYS)
        for variant in range(3)])
    tiles = tiles.reshape(3, N_GROUPS_B, HG_B // 2, 2 * QBLK_B, KBLK_B)
    return jnp.transpose(tiles, (2, 1, 0, 3, 4))


def _dilated_call(b0, b1, b2, bias):
    B = b0.shape[0]
    T = b0.shape[2]
    chunk = min(CHUNK_B, T)
    assert T % chunk == 0 and chunk % (QBLK_B * max(DILATIONS_B)) == 0
    assert T // max(DILATIONS_B) >= KBLK_B and DILATIONS_B[0] == 1
    in_specs, args = [], []
    for arr, dil in zip((b0, b1, b2), DILATIONS_B):
        sub_len = T // dil
        in_specs += [
            pl.BlockSpec((1, dil, chunk // dil, LANES), lambda b, p, j: (b, 0, j, p)),
            pl.BlockSpec((1, dil, sub_len, LANES), lambda b, p, j: (b, 0, 0, 2 + p)),
            pl.BlockSpec((1, dil, sub_len, LANES), lambda b, p, j: (b, 0, 0, 4 + p)),
        ]
        args += [arr, arr, arr]
    in_specs.append(pl.BlockSpec((1, N_GROUPS_B, 3, 2 * QBLK_B, KBLK_B),
                                 lambda b, p, j: (p, 0, 0, 0, 0)))
    args.append(bias)
    return pl.pallas_call(
        _dilated_kernel,
        grid=(B, HG_B // 2, T // chunk),
        in_specs=in_specs,
        out_specs=pl.BlockSpec((1, chunk, LANES), lambda b, p, j: (b, j, p)),
        out_shape=jax.ShapeDtypeStruct((B, T, GB_W), _BF16),
        scratch_shapes=[pltpu.VMEM((N_GROUPS_B, chunk, LANES), _F32)] * 3
        + [pltpu.VMEM((3, DEINT_STEP, chunk // DEINT_STEP, LANES), _F32)],
        compiler_params=pltpu.CompilerParams(
            dimension_semantics=("arbitrary", "arbitrary", "arbitrary"),
            vmem_limit_bytes=VMEM_LIMIT),
        name="dilated",
    )(*args)


def _mlp_kernel(x_ref, oa_ref, ob_ref, gate_ref, wa_ref, wb_ref, wo_ref, gmlp_ref, wup_ref,
                wdn_ref, y_ref):
    ya = jnp.dot(oa_ref[...], wa_ref[...], preferred_element_type=_F32)
    yb = jnp.dot(ob_ref[...], wb_ref[...], preferred_element_type=_F32)
    ga = gate_ref[:, :D_MODEL].astype(_F32)
    gb = gate_ref[:, D_MODEL:].astype(_F32)
    merged = (ga * ya + gb * yb).astype(_BF16)
    x1 = x_ref[...] + jnp.dot(merged, wo_ref[...], preferred_element_type=_F32)
    ms = jnp.mean(x1 * x1, axis=-1, keepdims=True)
    hm = (x1 * lax.rsqrt(ms + EPS) * gmlp_ref[...]).astype(_BF16)
    acc = x1
    for c in range(D_FF // FF_CHUNK):
        sl = slice(c * FF_CHUNK, (c + 1) * FF_CHUNK)
        u = jnp.maximum(jnp.dot(hm, wup_ref[:, sl], preferred_element_type=_F32), 0.0)
        acc = acc + jnp.dot((u * u).astype(_BF16), wdn_ref[sl, :], preferred_element_type=_F32)
    y_ref[...] = acc


def _mlp_call(x2, oa2, ob2, gates2, wa, wb, wo, gmlp, wup, wdn):
    n_tok, D = x2.shape
    tm = TM_MLP
    assert n_tok % tm == 0
    tok = lambda w: pl.BlockSpec((tm, w), lambda i: (i, 0))
    return pl.pallas_call(
        _mlp_kernel,
        grid=(n_tok // tm,),
        in_specs=[tok(D), tok(QA_W), tok(GB_W), tok(2 * D_MODEL),
                  _resident((QA_W, D)), _resident((GB_W, D)), _resident((D, D)),
                  _resident((1, D)), _resident((D, D_FF)), _resident((D_FF, D))],
        out_specs=tok(D),
        out_shape=jax.ShapeDtypeStruct((n_tok, D), _F32),
        compiler_params=pltpu.CompilerParams(
            dimension_semantics=("arbitrary",), vmem_limit_bytes=VMEM_LIMIT),
        name="mlp",
    )(x2, oa2, ob2, gates2, wa, wb, wo, gmlp, wup, wdn)


def _encoder_layer(x, p):
    B, T, D = x.shape
    qa, ka, va, b0, b1, b2, gates = _proj_call(
        x, p["gmix"], p["w_in"], p["gqa"], p["gka"], p["gqb"], p["gkb"])
    oa = _natten_call(qa, ka, va, p["bias_a"])
    ob = _dilated_call(b0, b1, b2, p["bias_b"])
    y = _mlp_call(x.reshape(B * T, D), oa.reshape(B * T, QA_W), ob.reshape(B * T, GB_W),
                  gates.reshape(B * T, 2 * D_MODEL), p["wa"], p["wb"], p["wo"], p["gmlp"],
                  p["wup"], p["wdn"])
    return y.reshape(B, T, D)


def _layer_params(norm_mix, w_in, q_norm_a, k_norm_a, q_norm_b, k_norm_b, rpb_a, t5_bias,
                  w_branch_a, w_branch_b, w_out, norm_mlp, w_up, w_down):
    tile_gain = lambda g: jnp.tile(g.astype(_F32), NORM_CHUNK // HEAD_DIM).reshape(1, NORM_CHUNK)
    return dict(
        gmix=norm_mix.astype(_F32).reshape(1, D_MODEL), w_in=w_in.astype(_BF16),
        gqa=tile_gain(q_norm_a), gka=tile_gain(k_norm_a),
        gqb=tile_gain(q_norm_b), gkb=tile_gain(k_norm_b),
        bias_a=_natten_bias(rpb_a), bias_b=_dilated_bias(t5_bias),
        wa=w_branch_a.astype(_BF16), wb=w_branch_b.astype(_BF16), wo=w_out.astype(_BF16),
        gmlp=norm_mlp.astype(_F32).reshape(1, D_MODEL),
        wup=w_up.astype(_BF16), wdn=w_down.astype(_BF16))


def kernel(x_prompt, x_sample, norm_mix, w_in, q_norm_a, k_norm_a, q_norm_b, k_norm_b, rpb_a,
           t5_bias, w_branch_a, w_branch_b, w_out, norm_mlp, w_up, w_down):
    y_prompt, y_sample = x_prompt, x_sample
    for l in range(norm_mix.shape[0]):
        p = _layer_params(norm_mix[l], w_in[l], q_norm_a[l], k_norm_a[l], q_norm_b[l],
                          k_norm_b[l], rpb_a[l], t5_bias, w_branch_a[l], w_branch_b[l],
                          w_out[l], norm_mlp[l], w_up[l], w_down[l])
        y_prompt = _encoder_layer(y_prompt, p)
        y_sample = _encoder_layer(y_sample, p)
    return (y_prompt, y_sample)
```

```python
import functools
import math

import numpy as np
import jax
import jax.numpy as jnp
from jax import lax
from jax.experimental import pallas as pl
from jax.experimental.pallas import tpu as pltpu

D_MODEL = 1024
HEAD_DIM = 64
H_A = 8
N_GROUPS_B = 3
HG_B = 4
H_B = N_GROUPS_B * HG_B
WINDOWS_B = (128, 512, 2048)
DILATIONS_B = (1, 4, 16)
GRID_W = 64
WIN_ROWS = 8
WIN_COLS = 16
NUM_BUCKETS = 32
T5_MAX_DIST = 1024
D_FF = 4 * D_MODEL
EPS = 1e-6
NEG = -1e30
LOG2E = math.log2(math.e)
QA_W = H_A * HEAD_DIM
QB_W = H_B * HEAD_DIM
GB_W = HG_B * HEAD_DIM
IN_W = 3 * QA_W + 3 * QB_W + 2 * D_MODEL
HALF_KEYS = 64
assert all((w // 2) // d == HALF_KEYS for w, d in zip(WINDOWS_B, DILATIONS_B))

LANES = 128
NORM_CHUNK = 256
TM_PROJ = 1024
DEINT_STEP = 4
PROJ_BLOCK = 1024
TM_MLP = 1024
FF_CHUNK = 1024
QROWS_A = 4
KROWS_A = QROWS_A + 8
BLOCKS_PER_STEP_A = 16
QBLK_B = 128
KBLK_B = QBLK_B + 2 * HALF_KEYS
CHUNK_B = 2048
VMEM_BYTES_V7X = 64 * 1024 * 1024
VMEM_LIMIT = VMEM_BYTES_V7X - 8 * 1024 * 1024
VMEM_LIMIT_PROJ = VMEM_BYTES_V7X - 4 * 1024 * 1024

_F32 = jnp.float32
_BF16 = jnp.bfloat16


def _resident(shape):
    nd = len(shape)
    return pl.BlockSpec(shape, lambda *_: (0,) * nd, pipeline_mode=pl.Buffered(1))


def _proj_kernel(x_ref, gmix_ref, w_ref, gqa_ref, gka_ref, gqb_ref, gkb_ref,
                 qa_ref, ka_ref, va_ref, b0_ref, b1_ref, b2_ref, gate_ref, scr_ref, scr2_ref):
    tm = x_ref.shape[1]
    x = x_ref[0]
    ms = jnp.mean(x * x, axis=-1, keepdims=True)
    h = (x * lax.rsqrt(ms + EPS) * gmix_ref[...]).astype(_BF16)

    wide = {}

    def chunk(c0):
        b0 = (c0 // PROJ_BLOCK) * PROJ_BLOCK
        if b0 not in wide:
            b1 = min(b0 + PROJ_BLOCK, IN_W)
            wide[b0] = jnp.dot(h, w_ref[:, b0:b1], preferred_element_type=_F32)
        return wide[b0][:, c0 - b0:c0 - b0 + NORM_CHUNK]

    lane = lax.broadcasted_iota(jnp.int32, (tm, LANES), 1)
    first = lane < HEAD_DIM

    def head_norm(t, gain_ref, scale):
        parts = []
        for c in range(t.shape[1] // LANES):
            tc = t[:, c * LANES:(c + 1) * LANES]
            sq = tc * tc
            s0 = jnp.sum(jnp.where(first, sq, 0.0), axis=-1, keepdims=True)
            s1 = jnp.sum(jnp.where(first, 0.0, sq), axis=-1, keepdims=True)
            msq = jnp.where(first, s0, s1) * (1.0 / HEAD_DIM)
            parts.append(tc * lax.rsqrt(msq + EPS))
        return jnp.concatenate(parts, axis=1) * (gain_ref[...] * scale)

    qk_scale = HEAD_DIM ** -0.5 * LOG2E

    for c in range(QA_W // NORM_CHUNK):
        sl = slice(c * NORM_CHUNK, (c + 1) * NORM_CHUNK)
        qa_ref[0, :, sl] = head_norm(chunk(sl.start), gqa_ref, qk_scale).astype(_BF16)
        ka_ref[0, :, sl] = head_norm(chunk(QA_W + sl.start), gka_ref, 1.0).astype(_BF16)
        va_ref[0, :, sl] = chunk(2 * QA_W + sl.start).astype(_BF16)

    outs = (b0_ref, b1_ref, b2_ref)
    base = 3 * QA_W
    slab = slab2 = 0
    for kind in range(3):
        for g, dil in enumerate(DILATIONS_B):
            tg = chunk(base + kind * QB_W + g * GB_W)
            if kind == 0:
                tg = head_norm(tg, gqb_ref, qk_scale)
            elif kind == 1:
                tg = head_norm(tg, gkb_ref, 1.0)
            lane0 = kind * GB_W
            if dil == 1:
                outs[g][0, 0, :, lane0:lane0 + GB_W] = tg.astype(_BF16)
                continue
            for half in range(GB_W // LANES):
                l0 = lane0 + half * LANES
                scr_ref[slab] = tg[:, half * LANES:(half + 1) * LANES]
                if dil == DEINT_STEP:
                    for r in range(dil):
                        sub = scr_ref[slab, pl.ds(r, tm // dil, stride=dil), :]
                        outs[g][0, r, :, l0:l0 + LANES] = sub.astype(_BF16)
                else:
                    assert dil == DEINT_STEP * DEINT_STEP
                    for lo in range(DEINT_STEP):
                        scr2_ref[slab2, lo] = scr_ref[
                            slab, pl.ds(lo, tm // DEINT_STEP, stride=DEINT_STEP), :]
                    for lo in range(DEINT_STEP):
                        for hi in range(DEINT_STEP):
                            sub = scr2_ref[slab2, lo, pl.ds(hi, tm // dil, stride=DEINT_STEP), :]
                            outs[g][0, hi * DEINT_STEP + lo, :, l0:l0 + LANES] = sub.astype(_BF16)
                    slab2 += 1
                slab += 1

    base = 3 * QA_W + 3 * QB_W
    for c in range(2 * D_MODEL // NORM_CHUNK):
        sl = slice(c * NORM_CHUNK, (c + 1) * NORM_CHUNK)
        gate = 0.5 * jnp.tanh(0.5 * chunk(base + sl.start)) + 0.5
        gate_ref[0, :, sl] = gate.astype(_BF16)


def _proj_call(x, gmix, w_in, gqa, gka, gqb, gkb):
    B, T, D = x.shape
    tm = TM_PROJ
    assert T % tm == 0 and tm % (16 * max(DILATIONS_B)) == 0
    n_slabs = 3 * sum(1 for d in DILATIONS_B if d > 1) * (GB_W // LANES)
    n_slabs2 = 3 * sum(1 for d in DILATIONS_B if d > DEINT_STEP) * (GB_W // LANES)
    tok = lambda w: pl.BlockSpec((1, tm, w), lambda b, i: (b, i, 0))
    sub = lambda d: pl.BlockSpec((1, d, tm // d, 3 * GB_W), lambda b, i: (b, 0, i, 0))
    out_shape = (
        jax.ShapeDtypeStruct((B, T, QA_W), _BF16),
        jax.ShapeDtypeStruct((B, T, QA_W), _BF16),
        jax.ShapeDtypeStruct((B, T, QA_W), _BF16),
    ) + tuple(jax.ShapeDtypeStruct((B, d, T // d, 3 * GB_W), _BF16) for d in DILATIONS_B) + (
        jax.ShapeDtypeStruct((B, T, 2 * D_MODEL), _BF16),
    )
    return pl.pallas_call(
        _proj_kernel,
        grid=(B, T // tm),
        in_specs=[tok(D), _resident((1, D)), _resident((D, IN_W)),
                  _resident((1, NORM_CHUNK)), _resident((1, NORM_CHUNK)),
                  _resident((1, NORM_CHUNK)), _resident((1, NORM_CHUNK))],
        out_specs=(tok(QA_W), tok(QA_W), tok(QA_W)) + tuple(sub(d) for d in DILATIONS_B)
        + (tok(2 * D_MODEL),),
        out_shape=out_shape,
        scratch_shapes=[pltpu.VMEM((n_slabs, tm, LANES), _F32),
                        pltpu.VMEM((n_slabs2, DEINT_STEP, tm // DEINT_STEP, LANES), _F32)],
        compiler_params=pltpu.CompilerParams(
            dimension_semantics=("arbitrary", "arbitrary"), vmem_limit_bytes=VMEM_LIMIT_PROJ),
        name="proj",
    )(x, gmix, w_in, gqa, gka, gqb, gkb)


def _pair_attention(q, kw, vw, bias2):
    m_rows = q.shape[0]
    lane = lax.broadcasted_iota(jnp.int32, (m_rows, LANES), 1)
    first = lane < HEAD_DIM
    zero = jnp.zeros_like(q)
    q2 = jnp.concatenate([jnp.where(first, q, zero), jnp.where(first, zero, q)], axis=0)
    s = lax.dot_general(q2, kw, (((1,), (1,)), ((), ())), preferred_element_type=_F32)
    s = s + bias2
    m = jnp.max(s, axis=-1, keepdims=True)
    p = jnp.exp2(s - m)
    l = jnp.sum(p, axis=-1, keepdims=True)
    o2 = jnp.dot(p.astype(_BF16), vw, preferred_element_type=_F32)

    def pick(x2):
        x2 = jnp.broadcast_to(x2, (2 * m_rows, LANES))
        return jnp.where(first, x2[:m_rows], x2[m_rows:])

    return pick(o2), pick(m), pick(l)


def _toeplitz(v, n_rows, n_cols, center):
    n = v.shape[-1]
    period = n_rows + n_cols - 1
    left = n_rows - 1 - center
    cfg = [(0, 0, 0)] * (v.ndim - 1) + [(left, period - n - left, 0)]
    ext = lax.pad(v, jnp.asarray(NEG, v.dtype), cfg)
    w = jnp.concatenate([ext[..., n_rows - 1:], ext[..., :n_rows - 1]], axis=-1)
    flat = jnp.tile(w, (1,) * (v.ndim - 1) + (n_rows,))[..., :n_rows * (period - 1)]
    return flat.reshape(v.shape[:-1] + (n_rows, period - 1))[..., :n_cols]


def _natten_kernel(q_ref, k_ref, v_ref, bias_ref, o_ref, *, rows):
    i = pl.program_id(2)
    mq, nk = QROWS_A * GRID_W, KROWS_A * GRID_W
    last_blk = rows // QROWS_A - 1
    blocks_per_step = q_ref.shape[1] // mq
    for u in range(blocks_per_step):
        blk = blocks_per_step * i + u
        krow0 = jnp.clip(QROWS_A * blk - WIN_ROWS // 2, 0, rows - KROWS_A)
        k0 = pl.multiple_of(krow0 * GRID_W, GRID_W)
        variant = jnp.where(blk == 0, 0, jnp.where(blk == last_blk, 2, 1))
        kw = k_ref[0, pl.ds(k0, nk), :]
        vw = v_ref[0, pl.ds(k0, nk), :]
        q = q_ref[0, u * mq:(u + 1) * mq, :]
        o, _, l = _pair_attention(q, kw, vw, bias_ref[variant, 0])
        o_ref[0, u * mq:(u + 1) * mq, :] = (o / l).astype(_BF16)


def _natten_bias(rpb):
    n_off = 2 * WIN_ROWS - 1
    toep = _toeplitz(rpb.astype(_F32) * LOG2E, GRID_W, GRID_W, WIN_COLS - 1)
    c = np.arange(GRID_W)[:, None]
    kc = np.arange(GRID_W)[None, :]
    col_start = np.clip(c - WIN_COLS // 2, 0, GRID_W - WIN_COLS)
    col_ok = (kc >= col_start) & (kc < col_start + WIN_COLS)
    col_bias = jnp.where(jnp.asarray(col_ok), toep, NEG)
    neg_block = jnp.full((H_A, GRID_W, GRID_W), NEG, _F32)
    fake_rows = 3 * KROWS_A
    nblk = fake_rows // QROWS_A
    variants = []
    for blk in (0, nblk // 2, nblk - 1):
        krow0 = int(np.clip(QROWS_A * blk - WIN_ROWS // 2, 0, fake_rows - KROWS_A))
        q_rows = []
        for rq in range(QROWS_A):
            r = QROWS_A * blk + rq
            start = int(np.clip(r - WIN_ROWS // 2, 0, fake_rows - WIN_ROWS))
            blocks = []
            for rk in range(KROWS_A):
                kr = krow0 + rk
                row_off = kr - r + WIN_ROWS - 1
                inside = start <= kr < start + WIN_ROWS
                assert not inside or 0 <= row_off < n_off
                blocks.append(col_bias[:, row_off] if inside else neg_block)
            q_rows.append(jnp.concatenate(blocks, axis=-1))
        variants.append(jnp.concatenate(q_rows, axis=-2))
    tiles = jnp.stack(variants)
    return tiles.reshape(3, H_A // 2, 2 * QROWS_A * GRID_W, KROWS_A * GRID_W)


def _natten_call(qa, ka, va, bias):
    B, T, _ = qa.shape
    rows = T // GRID_W
    rows_per_step = QROWS_A * min(BLOCKS_PER_STEP_A, rows // QROWS_A)
    assert T % GRID_W == 0 and rows % rows_per_step == 0 and rows >= KROWS_A
    mq, nk = rows_per_step * GRID_W, KROWS_A * GRID_W
    return pl.pallas_call(
        functools.partial(_natten_kernel, rows=rows),
        grid=(B, H_A // 2, rows // rows_per_step),
        in_specs=[pl.BlockSpec((1, mq, LANES), lambda b, p, i: (b, i, p)),
                  pl.BlockSpec((1, T, LANES), lambda b, p, i: (b, 0, p)),
                  pl.BlockSpec((1, T, LANES), lambda b, p, i: (b, 0, p)),
                  pl.BlockSpec((3, 1, 2 * QROWS_A * GRID_W, nk), lambda b, p, i: (0, p, 0, 0))],
        out_specs=pl.BlockSpec((1, mq, LANES), lambda b, p, i: (b, i, p)),
        out_shape=jax.ShapeDtypeStruct((B, T, QA_W), _BF16),
        compiler_params=pltpu.CompilerParams(
            dimension_semantics=("arbitrary", "arbitrary", "arbitrary"),
            vmem_limit_bytes=VMEM_LIMIT),
        name="natten",
    )(qa, ka, va, bias)


def _dilated_kernel(q0_ref, k0_ref, v0_ref, q1_ref, k1_ref, v1_ref, q2_ref, k2_ref, v2_ref,
                    bias_ref, o_ref, so_ref, sm_ref, sl_ref, stage_ref):
    j = pl.program_id(2)
    chunk = o_ref.shape[1]
    refs = ((q0_ref, k0_ref, v0_ref), (q1_ref, k1_ref, v1_ref), (q2_ref, k2_ref, v2_ref))

    def tile(g, r, sb):
        dil = DILATIONS_B[g]
        q_ref, k_ref, v_ref = refs[g]
        sub_len = k_ref.shape[2]
        lq0 = j * (chunk // dil) + sb * QBLK_B
        kl0 = jnp.clip(lq0 - HALF_KEYS, 0, sub_len - KBLK_B)
        variant = (lq0 - kl0) // HALF_KEYS
        kl0 = pl.multiple_of(kl0, HALF_KEYS)
        q = q_ref[0, r, sb * QBLK_B:(sb + 1) * QBLK_B, :]
        kw = k_ref[0, r, pl.ds(kl0, KBLK_B), :]
        vw = v_ref[0, r, pl.ds(kl0, KBLK_B), :]
        res = _pair_attention(q, kw, vw, bias_ref[0, g, variant])
        if dil > DEINT_STEP:
            lo, hi = r % DEINT_STEP, r // DEINT_STEP
            rows_out = pl.ds(sb * QBLK_B * DEINT_STEP + hi, QBLK_B, stride=DEINT_STEP)
            for a, val in enumerate(res):
                stage_ref[a, lo, rows_out, :] = val
            return
        if dil == 1:
            rows_out = pl.ds(sb * QBLK_B, QBLK_B)
        else:
            rows_out = pl.ds(sb * QBLK_B * dil + r, QBLK_B, stride=dil)
        for dst, val in zip((so_ref, sm_ref, sl_ref), res):
            dst[g, rows_out, :] = val

    def flush_stage(g, lo):
        rows_out = pl.ds(lo, chunk // DEINT_STEP, stride=DEINT_STEP)
        for a, dst in enumerate((so_ref, sm_ref, sl_ref)):
            dst[g, rows_out, :] = stage_ref[a, lo]

    def combine(rows):
        m_all = jnp.maximum(jnp.maximum(sm_ref[0, rows, :], sm_ref[1, rows, :]), sm_ref[2, rows, :])
        num = den = None
        for g in range(N_GROUPS_B):
            w = jnp.exp2(sm_ref[g, rows, :] - m_all)
            num = w * so_ref[g, rows, :] if num is None else num + w * so_ref[g, rows, :]
            den = w * sl_ref[g, rows, :] if den is None else den + w * sl_ref[g, rows, :]
        o_ref[0, rows, :] = (num / den).astype(_BF16)

    for g in reversed(range(N_GROUPS_B)):
        dil = DILATIONS_B[g]
        nsb = chunk // dil // QBLK_B
        if dil > DEINT_STEP:
            assert dil == DEINT_STEP * DEINT_STEP
            for lo in range(DEINT_STEP):
                for hi in range(DEINT_STEP):
                    for sb in range(nsb):
                        tile(g, hi * DEINT_STEP + lo, sb)
                flush_stage(g, lo)
            continue
        for r in range(dil):
            for sb in range(nsb):
                tile(g, r, sb)
                if dil == 1:
                    combine(slice(sb * QBLK_B, (sb + 1) * QBLK_B))


def _t5_buckets(rel):
    half = NUM_BUCKETS // 2
    ret = np.where(rel > 0, half, 0)
    n = np.abs(rel)
    max_exact = half // 2
    large = max_exact + (np.log(np.maximum(n, 1) / max_exact)
                         / np.log(T5_MAX_DIST / max_exact) * (half - max_exact)).astype(np.int32)
    large = np.minimum(large, half - 1)
    return (ret + np.where(n < max_exact, n, large)).astype(np.int32)


def _dilated_bias(t5_bias):
    rel = np.arange(-HALF_KEYS, HALF_KEYS + 1)
    tab = t5_bias.astype(_F32).T * LOG2E
    by_rel = jnp.stack([
        jnp.take(tab[g * HG_B:(g + 1) * HG_B], jnp.asarray(_t5_buckets(rel * dil)), axis=1)
        for g, dil in enumerate(DILATIONS_B)])
    tiles = jnp.stack([
        _toeplitz(by_rel, QBLK_B, KBLK_B, HALF_KEYS - variant * HALF_KEYS)
        for variant in range(3)])
    tiles = tiles.reshape(3, N_GROUPS_B, HG_B // 2, 2 * QBLK_B, KBLK_B)
    return jnp.transpose(tiles, (2, 1, 0, 3, 4))


def _dilated_call(b0, b1, b2, bias):
    B = b0.shape[0]
    T = b0.shape[2]
    chunk = min(CHUNK_B, T)
    assert T % chunk == 0 and chunk % (QBLK_B * max(DILATIONS_B)) == 0
    assert T // max(DILATIONS_B) >= KBLK_B and DILATIONS_B[0] == 1
    in_specs, args = [], []
    for arr, dil in zip((b0, b1, b2), DILATIONS_B):
        sub_len = T // dil
        in_specs += [
            pl.BlockSpec((1, dil, chunk // dil, LANES), lambda b, p, j: (b, 0, j, p)),
            pl.BlockSpec((1, dil, sub_len, LANES), lambda b, p, j: (b, 0, 0, 2 + p)),
            pl.BlockSpec((1, dil, sub_len, LANES), lambda b, p, j: (b, 0, 0, 4 + p)),
        ]
        args += [arr, arr, arr]
    in_specs.append(pl.BlockSpec((1, N_GROUPS_B, 3, 2 * QBLK_B, KBLK_B),
                                 lambda b, p, j: (p, 0, 0, 0, 0)))
    args.append(bias)
    return pl.pallas_call(
        _dilated_kernel,
        grid=(B, HG_B // 2, T // chunk),
        in_specs=in_specs,
        out_specs=pl.BlockSpec((1, chunk, LANES), lambda b, p, j: (b, j, p)),
        out_shape=jax.ShapeDtypeStruct((B, T, GB_W), _BF16),
        scratch_shapes=[pltpu.VMEM((N_GROUPS_B, chunk, LANES), _F32)] * 3
        + [pltpu.VMEM((3, DEINT_STEP, chunk // DEINT_STEP, LANES), _F32)],
        compiler_params=pltpu.CompilerParams(
            dimension_semantics=("arbitrary", "arbitrary", "arbitrary"),
            vmem_limit_bytes=VMEM_LIMIT),
        name="dilated",
    )(*args)


def _mlp_kernel(x_ref, oa_ref, ob_ref, gate_ref, wa_ref, wb_ref, wo_ref, gmlp_ref, wup_ref,
                wdn_ref, y_ref):
    ya = jnp.dot(oa_ref[...], wa_ref[...], preferred_element_type=_F32)
    yb = jnp.dot(ob_ref[...], wb_ref[...], preferred_element_type=_F32)
    ga = gate_ref[:, :D_MODEL].astype(_F32)
    gb = gate_ref[:, D_MODEL:].astype(_F32)
    merged = (ga * ya + gb * yb).astype(_BF16)
    x1 = x_ref[...] + jnp.dot(merged, wo_ref[...], preferred_element_type=_F32)
    ms = jnp.mean(x1 * x1, axis=-1, keepdims=True)
    hm = (x1 * lax.rsqrt(ms + EPS) * gmlp_ref[...]).astype(_BF16)
    acc = x1
    for c in range(D_FF // FF_CHUNK):
        sl = slice(c * FF_CHUNK, (c + 1) * FF_CHUNK)
        u = jnp.maximum(jnp.dot(hm, wup_ref[:, sl], preferred_element_type=_F32), 0.0)
        acc = acc + jnp.dot((u * u).astype(_BF16), wdn_ref[sl, :], preferred_element_type=_F32)
    y_ref[...] = acc


def _mlp_call(x2, oa2, ob2, gates2, wa, wb, wo, gmlp, wup, wdn):
    n_tok, D = x2.shape
    tm = TM_MLP
    assert n_tok % tm == 0
    tok = lambda w: pl.BlockSpec((tm, w), lambda i: (i, 0))
    return pl.pallas_call(
        _mlp_kernel,
        grid=(n_tok // tm,),
        in_specs=[tok(D), tok(QA_W), tok(GB_W), tok(2 * D_MODEL),
                  _resident((QA_W, D)), _resident((GB_W, D)), _resident((D, D)),
                  _resident((1, D)), _resident((D, D_FF)), _resident((D_FF, D))],
        out_specs=tok(D),
        out_shape=jax.ShapeDtypeStruct((n_tok, D), _F32),
        compiler_params=pltpu.CompilerParams(
            dimension_semantics=("arbitrary",), vmem_limit_bytes=VMEM_LIMIT_PROJ),
        name="mlp",
    )(x2, oa2, ob2, gates2, wa, wb, wo, gmlp, wup, wdn)


def _encoder_layer(x, p):
    B, T, D = x.shape
    qa, ka, va, b0, b1, b2, gates = _proj_call(
        x, p["gmix"], p["w_in"], p["gqa"], p["gka"], p["gqb"], p["gkb"])
    oa = _natten_call(qa, ka, va, p["bias_a"])
    ob = _dilated_call(b0, b1, b2, p["bias_b"])
    y = _mlp_call(x.reshape(B * T, D), oa.reshape(B * T, QA_W), ob.reshape(B * T, GB_W),
                  gates.reshape(B * T, 2 * D_MODEL), p["wa"], p["wb"], p["wo"], p["gmlp"],
                  p["wup"], p["wdn"])
    return y.reshape(B, T, D)


def _layer_params(norm_mix, w_in, q_norm_a, k_norm_a, q_norm_b, k_norm_b, rpb_a, t5_bias,
                  w_branch_a, w_branch_b, w_out, norm_mlp, w_up, w_down):
    tile_gain = lambda g: jnp.tile(g.astype(_F32), NORM_CHUNK // HEAD_DIM).reshape(1, NORM_CHUNK)
    return dict(
        gmix=norm_mix.astype(_F32).reshape(1, D_MODEL), w_in=w_in.astype(_BF16),
        gqa=tile_gain(q_norm_a), gka=tile_gain(k_norm_a),
        gqb=tile_gain(q_norm_b), gkb=tile_gain(k_norm_b),
        bias_a=_natten_bias(rpb_a), bias_b=_dilated_bias(t5_bias),
        wa=w_branch_a.astype(_BF16), wb=w_branch_b.astype(_BF16), wo=w_out.astype(_BF16),
        gmlp=norm_mlp.astype(_F32).reshape(1, D_MODEL),
        wup=w_up.astype(_BF16), wdn=w_down.astype(_BF16))


def kernel(x_prompt, x_sample, norm_mix, w_in, q_norm_a, k_norm_a, q_norm_b, k_norm_b, rpb_a,
           t5_bias, w_branch_a, w_branch_b, w_out, norm_mlp, w_up, w_down):
    y_prompt, y_sample = x_prompt, x_sample
    for l in range(norm_mix.shape[0]):
        p = _layer_params(norm_mix[l], w_in[l], q_norm_a[l], k_norm_a[l], q_norm_b[l],
                          k_norm_b[l], rpb_a[l], t5_bias, w_branch_a[l], w_branch_b[l],
                          w_out[l], norm_mlp[l], w_up[l], w_down[l])
        y_prompt = _encoder_layer(y_prompt, p)
        y_sample = _encoder_layer(y_sample, p)
    return (y_prompt, y_sample)
```

```python
import functools
import math

import numpy as np
import jax
import jax.numpy as jnp
from jax import lax
from jax.experimental import pallas as pl
from jax.experimental.pallas import tpu as pltpu

D_MODEL = 1024
HEAD_DIM = 64
H_A = 8
N_GROUPS_B = 3
HG_B = 4
H_B = N_GROUPS_B * HG_B
WINDOWS_B = (128, 512, 2048)
DILATIONS_B = (1, 4, 16)
GRID_W = 64
WIN_ROWS = 8
WIN_COLS = 16
NUM_BUCKETS = 32
T5_MAX_DIST = 1024
D_FF = 4 * D_MODEL
EPS = 1e-6
NEG = -1e30
LOG2E = math.log2(math.e)
QA_W = H_A * HEAD_DIM
QB_W = H_B * HEAD_DIM
GB_W = HG_B * HEAD_DIM
IN_W = 3 * QA_W + 3 * QB_W + 2 * D_MODEL
HALF_KEYS = 64
assert all((w // 2) // d == HALF_KEYS for w, d in zip(WINDOWS_B, DILATIONS_B))

LANES = 128
NORM_CHUNK = 256
TM_PROJ = 1024
DEINT_STEP = 4
PROJ_BLOCK = 1024
TM_MLP = 1024
FF_CHUNK = 1024
QROWS_A = 4
KROWS_A = QROWS_A + 8
BLOCKS_PER_STEP_A = 16
QBLK_B = 128
KBLK_B = QBLK_B + 2 * HALF_KEYS
CHUNK_B = 2048
VMEM_BYTES_V7X = 64 * 1024 * 1024
VMEM_LIMIT = VMEM_BYTES_V7X - 8 * 1024 * 1024
VMEM_LIMIT_DENSE = VMEM_BYTES_V7X - 4 * 1024 * 1024

_F32 = jnp.float32
_BF16 = jnp.bfloat16


def _resident(shape):
    nd = len(shape)
    return pl.BlockSpec(shape, lambda *_: (0,) * nd, pipeline_mode=pl.Buffered(1))


def _proj_kernel(x_ref, gmix_ref, w_ref, gqa_ref, gka_ref, gqb_ref, gkb_ref,
                 qa_ref, ka_ref, va_ref, b0_ref, b1_ref, b2_ref, gate_ref, scr_ref, scr2_ref):
    tm = x_ref.shape[1]
    x = x_ref[0]
    ms = jnp.mean(x * x, axis=-1, keepdims=True)
    h = (x * lax.rsqrt(ms + EPS) * gmix_ref[...]).astype(_BF16)

    wide = {}

    def chunk(c0):
        b0 = (c0 // PROJ_BLOCK) * PROJ_BLOCK
        if b0 not in wide:
            b1 = min(b0 + PROJ_BLOCK, IN_W)
            wide[b0] = jnp.dot(h, w_ref[:, b0:b1], preferred_element_type=_F32)
        return wide[b0][:, c0 - b0:c0 - b0 + NORM_CHUNK]

    lane = lax.broadcasted_iota(jnp.int32, (tm, LANES), 1)
    first = lane < HEAD_DIM

    def head_norm(t, gain_ref, scale):
        parts = []
        for c in range(t.shape[1] // LANES):
            tc = t[:, c * LANES:(c + 1) * LANES]
            sq = tc * tc
            s0 = jnp.sum(jnp.where(first, sq, 0.0), axis=-1, keepdims=True)
            s1 = jnp.sum(jnp.where(first, 0.0, sq), axis=-1, keepdims=True)
            msq = jnp.where(first, s0, s1) * (1.0 / HEAD_DIM)
            parts.append(tc * lax.rsqrt(msq + EPS))
        return jnp.concatenate(parts, axis=1) * (gain_ref[...] * scale)

    qk_scale = HEAD_DIM ** -0.5 * LOG2E

    for c in range(QA_W // NORM_CHUNK):
        sl = slice(c * NORM_CHUNK, (c + 1) * NORM_CHUNK)
        qa_ref[0, :, sl] = head_norm(chunk(sl.start), gqa_ref, qk_scale).astype(_BF16)
        ka_ref[0, :, sl] = head_norm(chunk(QA_W + sl.start), gka_ref, 1.0).astype(_BF16)
        va_ref[0, :, sl] = chunk(2 * QA_W + sl.start).astype(_BF16)

    outs = (b0_ref, b1_ref, b2_ref)
    base = 3 * QA_W
    slab = slab2 = 0
    for kind in range(3):
        for g, dil in enumerate(DILATIONS_B):
            tg = chunk(base + kind * QB_W + g * GB_W)
            if kind == 0:
                tg = head_norm(tg, gqb_ref, qk_scale)
            elif kind == 1:
                tg = head_norm(tg, gkb_ref, 1.0)
            lane0 = kind * GB_W
            if dil == 1:
                outs[g][0, 0, :, lane0:lane0 + GB_W] = tg.astype(_BF16)
                continue
            for half in range(GB_W // LANES):
                l0 = lane0 + half * LANES
                scr_ref[slab] = tg[:, half * LANES:(half + 1) * LANES]
                if dil == DEINT_STEP:
                    for r in range(dil):
                        sub = scr_ref[slab, pl.ds(r, tm // dil, stride=dil), :]
                        outs[g][0, r, :, l0:l0 + LANES] = sub.astype(_BF16)
                else:
                    assert dil == DEINT_STEP * DEINT_STEP
                    for lo in range(DEINT_STEP):
                        scr2_ref[slab2, lo] = scr_ref[
                            slab, pl.ds(lo, tm // DEINT_STEP, stride=DEINT_STEP), :]
                    for lo in range(DEINT_STEP):
                        for hi in range(DEINT_STEP):
                            sub = scr2_ref[slab2, lo, pl.ds(hi, tm // dil, stride=DEINT_STEP), :]
                            outs[g][0, hi * DEINT_STEP + lo, :, l0:l0 + LANES] = sub.astype(_BF16)
                    slab2 += 1
                slab += 1

    base = 3 * QA_W + 3 * QB_W
    for c in range(2 * D_MODEL // NORM_CHUNK):
        sl = slice(c * NORM_CHUNK, (c + 1) * NORM_CHUNK)
        gate = 0.5 * jnp.tanh(0.5 * chunk(base + sl.start)) + 0.5
        gate_ref[0, :, sl] = gate.astype(_BF16)


def _proj_call(x, gmix, w_in, gqa, gka, gqb, gkb):
    B, T, D = x.shape
    tm = TM_PROJ
    assert T % tm == 0 and tm % (16 * max(DILATIONS_B)) == 0
    n_slabs = 3 * sum(1 for d in DILATIONS_B if d > 1) * (GB_W // LANES)
    n_slabs2 = 3 * sum(1 for d in DILATIONS_B if d > DEINT_STEP) * (GB_W // LANES)
    tok = lambda w: pl.BlockSpec((1, tm, w), lambda b, i: (b, i, 0))
    sub = lambda d: pl.BlockSpec((1, d, tm // d, 3 * GB_W), lambda b, i: (b, 0, i, 0))
    out_shape = (
        jax.ShapeDtypeStruct((B, T, QA_W), _BF16),
        jax.ShapeDtypeStruct((B, T, QA_W), _BF16),
        jax.ShapeDtypeStruct((B, T, QA_W), _BF16),
    ) + tuple(jax.ShapeDtypeStruct((B, d, T // d, 3 * GB_W), _BF16) for d in DILATIONS_B) + (
        jax.ShapeDtypeStruct((B, T, 2 * D_MODEL), _BF16),
    )
    return pl.pallas_call(
        _proj_kernel,
        grid=(B, T // tm),
        in_specs=[tok(D), _resident((1, D)), _resident((D, IN_W)),
                  _resident((1, NORM_CHUNK)), _resident((1, NORM_CHUNK)),
                  _resident((1, NORM_CHUNK)), _resident((1, NORM_CHUNK))],
        out_specs=(tok(QA_W), tok(QA_W), tok(QA_W)) + tuple(sub(d) for d in DILATIONS_B)
        + (tok(2 * D_MODEL),),
        out_shape=out_shape,
        scratch_shapes=[pltpu.VMEM((n_slabs, tm, LANES), _F32),
                        pltpu.VMEM((n_slabs2, DEINT_STEP, tm // DEINT_STEP, LANES), _F32)],
        compiler_params=pltpu.CompilerParams(
            dimension_semantics=("arbitrary", "arbitrary"), vmem_limit_bytes=VMEM_LIMIT_DENSE),
        name="proj",
    )(x, gmix, w_in, gqa, gka, gqb, gkb)


def _pair_attention(q, kw, vw, bias2):
    m_rows = q.shape[0]
    lane = lax.broadcasted_iota(jnp.int32, (m_rows, LANES), 1)
    first = lane < HEAD_DIM
    zero = jnp.zeros_like(q)
    q2 = jnp.concatenate([jnp.where(first, q, zero), jnp.where(first, zero, q)], axis=0)
    s = lax.dot_general(q2, kw, (((1,), (1,)), ((), ())), preferred_element_type=_F32)
    s = s + bias2
    m = jnp.max(s, axis=-1, keepdims=True)
    p = jnp.exp2(s - m)
    l = jnp.sum(p, axis=-1, keepdims=True)
    o2 = jnp.dot(p.astype(_BF16), vw, preferred_element_type=_F32)

    def pick(x2):
        x2 = jnp.broadcast_to(x2, (2 * m_rows, LANES))
        return jnp.where(first, x2[:m_rows], x2[m_rows:])

    return pick(o2), pick(m), pick(l)


def _toeplitz(v, n_rows, n_cols, center):
    n = v.shape[-1]
    period = n_rows + n_cols - 1
    left = n_rows - 1 - center
    cfg = [(0, 0, 0)] * (v.ndim - 1) + [(left, period - n - left, 0)]
    ext = lax.pad(v, jnp.asarray(NEG, v.dtype), cfg)
    w = jnp.concatenate([ext[..., n_rows - 1:], ext[..., :n_rows - 1]], axis=-1)
    flat = jnp.tile(w, (1,) * (v.ndim - 1) + (n_rows,))[..., :n_rows * (period - 1)]
    return flat.reshape(v.shape[:-1] + (n_rows, period - 1))[..., :n_cols]


def _natten_kernel(q_ref, k_ref, v_ref, bias_ref, o_ref, *, rows):
    i = pl.program_id(2)
    mq, nk = QROWS_A * GRID_W, KROWS_A * GRID_W
    last_blk = rows // QROWS_A - 1
    blocks_per_step = q_ref.shape[1] // mq
    for u in range(blocks_per_step):
        blk = blocks_per_step * i + u
        krow0 = jnp.clip(QROWS_A * blk - WIN_ROWS // 2, 0, rows - KROWS_A)
        k0 = pl.multiple_of(krow0 * GRID_W, GRID_W)
        variant = jnp.where(blk == 0, 0, jnp.where(blk == last_blk, 2, 1))
        kw = k_ref[0, pl.ds(k0, nk), :]
        vw = v_ref[0, pl.ds(k0, nk), :]
        q = q_ref[0, u * mq:(u + 1) * mq, :]
        o, _, l = _pair_attention(q, kw, vw, bias_ref[variant, 0])
        o_ref[0, u * mq:(u + 1) * mq, :] = (o / l).astype(_BF16)


def _natten_bias(rpb):
    n_off = 2 * WIN_ROWS - 1
    toep = _toeplitz(rpb.astype(_F32) * LOG2E, GRID_W, GRID_W, WIN_COLS - 1)
    c = np.arange(GRID_W)[:, None]
    kc = np.arange(GRID_W)[None, :]
    col_start = np.clip(c - WIN_COLS // 2, 0, GRID_W - WIN_COLS)
    col_ok = (kc >= col_start) & (kc < col_start + WIN_COLS)
    col_bias = jnp.where(jnp.asarray(col_ok), toep, NEG)
    neg_block = jnp.full((H_A, GRID_W, GRID_W), NEG, _F32)
    fake_rows = 3 * KROWS_A
    nblk = fake_rows // QROWS_A
    variants = []
    for blk in (0, nblk // 2, nblk - 1):
        krow0 = int(np.clip(QROWS_A * blk - WIN_ROWS // 2, 0, fake_rows - KROWS_A))
        q_rows = []
        for rq in range(QROWS_A):
            r = QROWS_A * blk + rq
            start = int(np.clip(r - WIN_ROWS // 2, 0, fake_rows - WIN_ROWS))
            blocks = []
            for rk in range(KROWS_A):
                kr = krow0 + rk
                row_off = kr - r + WIN_ROWS - 1
                inside = start <= kr < start + WIN_ROWS
                assert not inside or 0 <= row_off < n_off
                blocks.append(col_bias[:, row_off] if inside else neg_block)
            q_rows.append(jnp.concatenate(blocks, axis=-1))
        variants.append(jnp.concatenate(q_rows, axis=-2))
    tiles = jnp.stack(variants)
    return tiles.reshape(3, H_A // 2, 2 * QROWS_A * GRID_W, KROWS_A * GRID_W)


def _natten_call(qa, ka, va, bias):
    B, T, _ = qa.shape
    rows = T // GRID_W
    rows_per_step = QROWS_A * min(BLOCKS_PER_STEP_A, rows // QROWS_A)
    assert T % GRID_W == 0 and rows % rows_per_step == 0 and rows >= KROWS_A
    mq, nk = rows_per_step * GRID_W, KROWS_A * GRID_W
    return pl.pallas_call(
        functools.partial(_natten_kernel, rows=rows),
        grid=(B, H_A // 2, rows // rows_per_step),
        in_specs=[pl.BlockSpec((1, mq, LANES), lambda b, p, i: (b, i, p)),
                  pl.BlockSpec((1, T, LANES), lambda b, p, i: (b, 0, p)),
                  pl.BlockSpec((1, T, LANES), lambda b, p, i: (b, 0, p)),
                  pl.BlockSpec((3, 1, 2 * QROWS_A * GRID_W, nk), lambda b, p, i: (0, p, 0, 0))],
        out_specs=pl.BlockSpec((1, mq, LANES), lambda b, p, i: (b, i, p)),
        out_shape=jax.ShapeDtypeStruct((B, T, QA_W), _BF16),
        compiler_params=pltpu.CompilerParams(
            dimension_semantics=("arbitrary", "arbitrary", "arbitrary"),
            vmem_limit_bytes=VMEM_LIMIT),
        name="natten",
    )(qa, ka, va, bias)


def _dilated_kernel(q0_ref, k0_ref, v0_ref, q1_ref, k1_ref, v1_ref, q2_ref, k2_ref, v2_ref,
                    bias_ref, o_ref, so_ref, sm_ref, sl_ref, stage_ref):
    j = pl.program_id(2)
    chunk = o_ref.shape[1]
    refs = ((q0_ref, k0_ref, v0_ref), (q1_ref, k1_ref, v1_ref), (q2_ref, k2_ref, v2_ref))

    def tile(g, r, sb):
        dil = DILATIONS_B[g]
        q_ref, k_ref, v_ref = refs[g]
        sub_len = k_ref.shape[2]
        lq0 = j * (chunk // dil) + sb * QBLK_B
        kl0 = jnp.clip(lq0 - HALF_KEYS, 0, sub_len - KBLK_B)
        variant = (lq0 - kl0) // HALF_KEYS
        kl0 = pl.multiple_of(kl0, HALF_KEYS)
        q = q_ref[0, r, sb * QBLK_B:(sb + 1) * QBLK_B, :]
        kw = k_ref[0, r, pl.ds(kl0, KBLK_B), :]
        vw = v_ref[0, r, pl.ds(kl0, KBLK_B), :]
        res = _pair_attention(q, kw, vw, bias_ref[0, g, variant])
        if dil > DEINT_STEP:
            lo, hi = r % DEINT_STEP, r // DEINT_STEP
            rows_out = pl.ds(sb * QBLK_B * DEINT_STEP + hi, QBLK_B, stride=DEINT_STEP)
            for a, val in enumerate(res):
                stage_ref[a, lo, rows_out, :] = val
            return
        if dil == 1:
            rows_out = pl.ds(sb * QBLK_B, QBLK_B)
        else:
            rows_out = pl.ds(sb * QBLK_B * dil + r, QBLK_B, stride=dil)
        for dst, val in zip((so_ref, sm_ref, sl_ref), res):
            dst[g, rows_out, :] = val

    def flush_stage(g, lo):
        rows_out = pl.ds(lo, chunk // DEINT_STEP, stride=DEINT_STEP)
        for a, dst in enumerate((so_ref, sm_ref, sl_ref)):
            dst[g, rows_out, :] = stage_ref[a, lo]

    def combine(rows):
        m_all = jnp.maximum(jnp.maximum(sm_ref[0, rows, :], sm_ref[1, rows, :]), sm_ref[2, rows, :])
        num = den = None
        for g in range(N_GROUPS_B):
            w = jnp.exp2(sm_ref[g, rows, :] - m_all)
            num = w * so_ref[g, rows, :] if num is None else num + w * so_ref[g, rows, :]
            den = w * sl_ref[g, rows, :] if den is None else den + w * sl_ref[g, rows, :]
        o_ref[0, rows, :] = (num / den).astype(_BF16)

    for g in reversed(range(N_GROUPS_B)):
        dil = DILATIONS_B[g]
        nsb = chunk // dil // QBLK_B
        if dil > DEINT_STEP:
            assert dil == DEINT_STEP * DEINT_STEP
            for lo in range(DEINT_STEP):
                for hi in range(DEINT_STEP):
                    for sb in range(nsb):
                        tile(g, hi * DEINT_STEP + lo, sb)
                flush_stage(g, lo)
            continue
        for r in range(dil):
            for sb in range(nsb):
                tile(g, r, sb)
                if dil == 1:
                    combine(slice(sb * QBLK_B, (sb + 1) * QBLK_B))


def _t5_buckets(rel):
    half = NUM_BUCKETS // 2
    ret = np.where(rel > 0, half, 0)
    n = np.abs(rel)
    max_exact = half // 2
    large = max_exact + (np.log(np.maximum(n, 1) / max_exact)
                         / np.log(T5_MAX_DIST / max_exact) * (half - max_exact)).astype(np.int32)
    large = np.minimum(large, half - 1)
    return (ret + np.where(n < max_exact, n, large)).astype(np.int32)


def _dilated_bias(t5_bias):
    rel = np.arange(-HALF_KEYS, HALF_KEYS + 1)
    tab = t5_bias.astype(_F32).T * LOG2E
    by_rel = jnp.stack([
        jnp.take(tab[g * HG_B:(g + 1) * HG_B], jnp.asarray(_t5_buckets(rel * dil)), axis=1)
        for g, dil in enumerate(DILATIONS_B)])
    tiles = jnp.stack([
        _toeplitz(by_rel, QBLK_B, KBLK_B, HALF_KEYS - variant * HALF_KEYS)
        for variant in range(3)])
    tiles = tiles.reshape(3, N_GROUPS_B, HG_B // 2, 2 * QBLK_B, KBLK_B)
    return jnp.transpose(tiles, (2, 1, 0, 3, 4))


def _dilated_call(b0, b1, b2, bias):
    B = b0.shape[0]
    T = b0.shape[2]
    chunk = min(CHUNK_B, T)
    assert T % chunk == 0 and chunk % (QBLK_B * max(DILATIONS_B)) == 0
    assert T // max(DILATIONS_B) >= KBLK_B and DILATIONS_B[0] == 1
    in_specs, args = [], []
    for arr, dil in zip((b0, b1, b2), DILATIONS_B):
        sub_len = T // dil
        in_specs += [
            pl.BlockSpec((1, dil, chunk // dil, LANES), lambda b, p, j: (b, 0, j, p)),
            pl.BlockSpec((1, dil, sub_len, LANES), lambda b, p, j: (b, 0, 0, 2 + p)),
            pl.BlockSpec((1, dil, sub_len, LANES), lambda b, p, j: (b, 0, 0, 4 + p)),
        ]
        args += [arr, arr, arr]
    in_specs.append(pl.BlockSpec((1, N_GROUPS_B, 3, 2 * QBLK_B, KBLK_B),
                                 lambda b, p, j: (p, 0, 0, 0, 0)))
    args.append(bias)
    return pl.pallas_call(
        _dilated_kernel,
        grid=(B, HG_B // 2, T // chunk),
        in_specs=in_specs,
        out_specs=pl.BlockSpec((1, chunk, LANES), lambda b, p, j: (b, j, p)),
        out_shape=jax.ShapeDtypeStruct((B, T, GB_W), _BF16),
        scratch_shapes=[pltpu.VMEM((N_GROUPS_B, chunk, LANES), _F32)] * 3
        + [pltpu.VMEM((3, DEINT_STEP, chunk // DEINT_STEP, LANES), _F32)],
        compiler_params=pltpu.CompilerParams(
            dimension_semantics=("arbitrary", "arbitrary", "arbitrary"),
            vmem_limit_bytes=VMEM_LIMIT),
        name="dilated",
    )(*args)


def _mlp_kernel(x_ref, oa_ref, ob_ref, gate_ref, wa_ref, wb_ref, wo_ref, gmlp_ref, wup_ref,
                wdn_ref, y_ref):
    ya = jnp.dot(oa_ref[...], wa_ref[...], preferred_element_type=_F32)
    yb = jnp.dot(ob_ref[...], wb_ref[...], preferred_element_type=_F32)
    ga = gate_ref[:, :D_MODEL].astype(_F32)
    gb = gate_ref[:, D_MODEL:].astype(_F32)
    merged = (ga * ya + gb * yb).astype(_BF16)
    x1 = x_ref[...] + jnp.dot(merged, wo_ref[...], preferred_element_type=_F32)
    ms = jnp.mean(x1 * x1, axis=-1, keepdims=True)
    hm = (x1 * lax.rsqrt(ms + EPS) * gmlp_ref[...]).astype(_BF16)
    acc = x1
    for c in range(D_FF // FF_CHUNK):
        sl = slice(c * FF_CHUNK, (c + 1) * FF_CHUNK)
        u = jnp.maximum(jnp.dot(hm, wup_ref[:, sl], preferred_element_type=_F32), 0.0)
        acc = acc + jnp.dot((u * u).astype(_BF16), wdn_ref[sl, :], preferred_element_type=_F32)
    y_ref[...] = acc


def _mlp_call(x2, oa2, ob2, gates2, wa, wb, wo, gmlp, wup, wdn):
    n_tok, D = x2.shape
    tm = TM_MLP
    assert n_tok % tm == 0
    tok = lambda w: pl.BlockSpec((tm, w), lambda i: (i, 0))
    return pl.pallas_call(
        _mlp_kernel,
        grid=(n_tok // tm,),
        in_specs=[tok(D), tok(QA_W), tok(GB_W), tok(2 * D_MODEL),
                  _resident((QA_W, D)), _resident((GB_W, D)), _resident((D, D)),
                  _resident((1, D)), _resident((D, D_FF)), _resident((D_FF, D))],
        out_specs=tok(D),
        out_shape=jax.ShapeDtypeStruct((n_tok, D), _F32),
        compiler_params=pltpu.CompilerParams(
            dimension_semantics=("arbitrary",), vmem_limit_bytes=VMEM_LIMIT_DENSE),
        name="mlp",
    )(x2, oa2, ob2, gates2, wa, wb, wo, gmlp, wup, wdn)


def _encoder_layer(x, p):
    B, T, D = x.shape
    qa, ka, va, b0, b1, b2, gates = _proj_call(
        x, p["gmix"], p["w_in"], p["gqa"], p["gka"], p["gqb"], p["gkb"])
    oa = _natten_call(qa, ka, va, p["bias_a"])
    ob = _dilated_call(b0, b1, b2, p["bias_b"])
    y = _mlp_call(x.reshape(B * T, D), oa.reshape(B * T, QA_W), ob.reshape(B * T, GB_W),
                  gates.reshape(B * T, 2 * D_MODEL), p["wa"], p["wb"], p["wo"], p["gmlp"],
                  p["wup"], p["wdn"])
    return y.reshape(B, T, D)


def _layer_params(norm_mix, w_in, q_norm_a, k_norm_a, q_norm_b, k_norm_b, rpb_a, t5_bias,
                  w_branch_a, w_branch_b, w_out, norm_mlp, w_up, w_down):
    tile_gain = lambda g: jnp.tile(g.astype(_F32), NORM_CHUNK // HEAD_DIM).reshape(1, NORM_CHUNK)
    return dict(
        gmix=norm_mix.astype(_F32).reshape(1, D_MODEL), w_in=w_in.astype(_BF16),
        gqa=tile_gain(q_norm_a), gka=tile_gain(k_norm_a),
        gqb=tile_gain(q_norm_b), gkb=tile_gain(k_norm_b),
        bias_a=_natten_bias(rpb_a), bias_b=_dilated_bias(t5_bias),
        wa=w_branch_a.astype(_BF16), wb=w_branch_b.astype(_BF16), wo=w_out.astype(_BF16),
        gmlp=norm_mlp.astype(_F32).reshape(1, D_MODEL),
        wup=w_up.astype(_BF16), wdn=w_down.astype(_BF16))


def kernel(x_prompt, x_sample, norm_mix, w_in, q_norm_a, k_norm_a, q_norm_b, k_norm_b, rpb_a,
           t5_bias, w_branch_a, w_branch_b, w_out, norm_mlp, w_up, w_down):
    y_prompt, y_sample = x_prompt, x_sample
    for l in range(norm_mix.shape[0]):
        p = _layer_params(norm_mix[l], w_in[l], q_norm_a[l], k_norm_a[l], q_norm_b[l],
                          k_norm_b[l], rpb_a[l], t5_bias, w_branch_a[l], w_branch_b[l],
                          w_out[l], norm_mlp[l], w_up[l], w_down[l])
        y_prompt = _encoder_layer(y_prompt, p)
        y_sample = _encoder_layer(y_sample, p)
    return (y_prompt, y_sample)
```

```python
import functools
import math

import numpy as np
import jax
import jax.numpy as jnp
from jax import lax
from jax.experimental import pallas as pl
from jax.experimental.pallas import tpu as pltpu

D_MODEL = 1024
HEAD_DIM = 64
H_A = 8
N_GROUPS_B = 3
HG_B = 4
H_B = N_GROUPS_B * HG_B
WINDOWS_B = (128, 512, 2048)
DILATIONS_B = (1, 4, 16)
GRID_W = 64
WIN_ROWS = 8
WIN_COLS = 16
NUM_BUCKETS = 32
T5_MAX_DIST = 1024
D_FF = 4 * D_MODEL
EPS = 1e-6
NEG = -1e30
LOG2E = math.log2(math.e)
QA_W = H_A * HEAD_DIM
QB_W = H_B * HEAD_DIM
GB_W = HG_B * HEAD_DIM
IN_W = 3 * QA_W + 3 * QB_W + 2 * D_MODEL
HALF_KEYS = 64
assert all((w // 2) // d == HALF_KEYS for w, d in zip(WINDOWS_B, DILATIONS_B))

LANES = 128
NORM_CHUNK = 256
TM_PROJ = 1024
DEINT_STEP = 4
PROJ_BLOCK = 1024
TM_MLP = 1024
FF_CHUNK = 1024
QROWS_A = 4
KROWS_A = QROWS_A + 8
BLOCKS_PER_STEP_A = 16
QBLK_B = 128
KBLK_B = QBLK_B + 2 * HALF_KEYS
CHUNK_B = 2048
VMEM_BYTES_V7X = 64 * 1024 * 1024
VMEM_LIMIT = VMEM_BYTES_V7X - 8 * 1024 * 1024
VMEM_LIMIT_DENSE = VMEM_BYTES_V7X - 4 * 1024 * 1024

_F32 = jnp.float32
_BF16 = jnp.bfloat16


def _resident(shape):
    nd = len(shape)
    return pl.BlockSpec(shape, lambda *_: (0,) * nd, pipeline_mode=pl.Buffered(1))


def _proj_kernel(x_ref, gmix_ref, w_ref, gqa_ref, gka_ref, gqb_ref, gkb_ref,
                 qa_ref, ka_ref, va_ref, b0_ref, b1_ref, b2_ref, gate_ref, scr_ref, scr2_ref):
    tm = x_ref.shape[1]
    x = x_ref[0]
    ms = jnp.mean(x * x, axis=-1, keepdims=True)
    h = (x * lax.rsqrt(ms + EPS) * gmix_ref[...]).astype(_BF16)

    wide = {}

    def chunk(c0):
        b0 = (c0 // PROJ_BLOCK) * PROJ_BLOCK
        if b0 not in wide:
            b1 = min(b0 + PROJ_BLOCK, IN_W)
            wide[b0] = jnp.dot(h, w_ref[:, b0:b1], preferred_element_type=_F32)
        return wide[b0][:, c0 - b0:c0 - b0 + NORM_CHUNK]

    lane = lax.broadcasted_iota(jnp.int32, (tm, LANES), 1)
    first = lane < HEAD_DIM

    def head_norm(t, gain_ref, scale):
        parts = []
        for c in range(t.shape[1] // LANES):
            tc = t[:, c * LANES:(c + 1) * LANES]
            sq = tc * tc
            s0 = jnp.sum(jnp.where(first, sq, 0.0), axis=-1, keepdims=True)
            s1 = jnp.sum(jnp.where(first, 0.0, sq), axis=-1, keepdims=True)
            msq = jnp.where(first, s0, s1) * (1.0 / HEAD_DIM)
            parts.append(tc * lax.rsqrt(msq + EPS))
        return jnp.concatenate(parts, axis=1) * (gain_ref[...] * scale)

    qk_scale = HEAD_DIM ** -0.5 * LOG2E

    for c in range(QA_W // NORM_CHUNK):
        sl = slice(c * NORM_CHUNK, (c + 1) * NORM_CHUNK)
        qa_ref[0, :, sl] = head_norm(chunk(sl.start), gqa_ref, qk_scale).astype(_BF16)
        ka_ref[0, :, sl] = head_norm(chunk(QA_W + sl.start), gka_ref, 1.0).astype(_BF16)
        va_ref[0, :, sl] = chunk(2 * QA_W + sl.start).astype(_BF16)

    outs = (b0_ref, b1_ref, b2_ref)
    base = 3 * QA_W
    slab = slab2 = 0
    for kind in range(3):
        for g, dil in enumerate(DILATIONS_B):
            tg = chunk(base + kind * QB_W + g * GB_W)
            if kind == 0:
                tg = head_norm(tg, gqb_ref, qk_scale)
            elif kind == 1:
                tg = head_norm(tg, gkb_ref, 1.0)
            lane0 = kind * GB_W
            if dil == 1:
                outs[g][0, 0, :, lane0:lane0 + GB_W] = tg.astype(_BF16)
                continue
            for half in range(GB_W // LANES):
                l0 = lane0 + half * LANES
                scr_ref[slab] = tg[:, half * LANES:(half + 1) * LANES]
                if dil == DEINT_STEP:
                    for r in range(dil):
                        sub = scr_ref[slab, pl.ds(r, tm // dil, stride=dil), :]
                        outs[g][0, r, :, l0:l0 + LANES] = sub.astype(_BF16)
                else:
                    assert dil == DEINT_STEP * DEINT_STEP
                    for lo in range(DEINT_STEP):
                        scr2_ref[slab2, lo] = scr_ref[
                            slab, pl.ds(lo, tm // DEINT_STEP, stride=DEINT_STEP), :]
                    for lo in range(DEINT_STEP):
                        for hi in range(DEINT_STEP):
                            sub = scr2_ref[slab2, lo, pl.ds(hi, tm // dil, stride=DEINT_STEP), :]
                            outs[g][0, hi * DEINT_STEP + lo, :, l0:l0 + LANES] = sub.astype(_BF16)
                    slab2 += 1
                slab += 1

    base = 3 * QA_W + 3 * QB_W
    for c in range(2 * D_MODEL // NORM_CHUNK):
        sl = slice(c * NORM_CHUNK, (c + 1) * NORM_CHUNK)
        gate = 0.5 * jnp.tanh(0.5 * chunk(base + sl.start)) + 0.5
        gate_ref[0, :, sl] = gate.astype(_BF16)


def _proj_call(x, gmix, w_in, gqa, gka, gqb, gkb):
    B, T, D = x.shape
    tm = TM_PROJ
    assert T % tm == 0 and tm % (16 * max(DILATIONS_B)) == 0
    n_slabs = 3 * sum(1 for d in DILATIONS_B if d > 1) * (GB_W // LANES)
    n_slabs2 = 3 * sum(1 for d in DILATIONS_B if d > DEINT_STEP) * (GB_W // LANES)
    tok = lambda w: pl.BlockSpec((1, tm, w), lambda b, i: (b, i, 0))
    sub = lambda d: pl.BlockSpec((1, d, tm // d, 3 * GB_W), lambda b, i: (b, 0, i, 0))
    out_shape = (
        jax.ShapeDtypeStruct((B, T, QA_W), _BF16),
        jax.ShapeDtypeStruct((B, T, QA_W), _BF16),
        jax.ShapeDtypeStruct((B, T, QA_W), _BF16),
    ) + tuple(jax.ShapeDtypeStruct((B, d, T // d, 3 * GB_W), _BF16) for d in DILATIONS_B) + (
        jax.ShapeDtypeStruct((B, T, 2 * D_MODEL), _BF16),
    )
    return pl.pallas_call(
        _proj_kernel,
        grid=(B, T // tm),
        in_specs=[tok(D), _resident((1, D)), _resident((D, IN_W)),
                  _resident((1, NORM_CHUNK)), _resident((1, NORM_CHUNK)),
                  _resident((1, NORM_CHUNK)), _resident((1, NORM_CHUNK))],
        out_specs=(tok(QA_W), tok(QA_W), tok(QA_W)) + tuple(sub(d) for d in DILATIONS_B)
        + (tok(2 * D_MODEL),),
        out_shape=out_shape,
        scratch_shapes=[pltpu.VMEM((n_slabs, tm, LANES), _F32),
                        pltpu.VMEM((n_slabs2, DEINT_STEP, tm // DEINT_STEP, LANES), _F32)],
        compiler_params=pltpu.CompilerParams(
            dimension_semantics=("arbitrary", "arbitrary"), vmem_limit_bytes=VMEM_LIMIT_DENSE),
        name="proj",
    )(x, gmix, w_in, gqa, gka, gqb, gkb)


def _pair_attention(q, kw, vw, bias2):
    m_rows = q.shape[0]
    lane = lax.broadcasted_iota(jnp.int32, (m_rows, LANES), 1)
    first = lane < HEAD_DIM
    zero = jnp.zeros_like(q)
    q2 = jnp.concatenate([jnp.where(first, q, zero), jnp.where(first, zero, q)], axis=0)
    s = lax.dot_general(q2, kw, (((1,), (1,)), ((), ())), preferred_element_type=_F32)
    s = s + bias2
    m = jnp.max(s, axis=-1, keepdims=True)
    p = jnp.exp2(s - m)
    o2l = jnp.dot(p.astype(_BF16), jnp.concatenate([vw, jnp.ones_like(vw)], axis=1),
                  preferred_element_type=_F32)
    o2, l = o2l[:, :LANES], o2l[:, LANES:]

    def pick(x2):
        x2 = jnp.broadcast_to(x2, (2 * m_rows, LANES))
        return jnp.where(first, x2[:m_rows], x2[m_rows:])

    return pick(o2), pick(m), pick(l)


def _toeplitz(v, n_rows, n_cols, center):
    n = v.shape[-1]
    period = n_rows + n_cols - 1
    left = n_rows - 1 - center
    cfg = [(0, 0, 0)] * (v.ndim - 1) + [(left, period - n - left, 0)]
    ext = lax.pad(v, jnp.asarray(NEG, v.dtype), cfg)
    w = jnp.concatenate([ext[..., n_rows - 1:], ext[..., :n_rows - 1]], axis=-1)
    flat = jnp.tile(w, (1,) * (v.ndim - 1) + (n_rows,))[..., :n_rows * (period - 1)]
    return flat.reshape(v.shape[:-1] + (n_rows, period - 1))[..., :n_cols]


def _natten_kernel(q_ref, k_ref, v_ref, bias_ref, o_ref, *, rows):
    i = pl.program_id(2)
    mq, nk = QROWS_A * GRID_W, KROWS_A * GRID_W
    last_blk = rows // QROWS_A - 1
    blocks_per_step = q_ref.shape[1] // mq
    for u in range(blocks_per_step):
        blk = blocks_per_step * i + u
        krow0 = jnp.clip(QROWS_A * blk - WIN_ROWS // 2, 0, rows - KROWS_A)
        k0 = pl.multiple_of(krow0 * GRID_W, GRID_W)
        variant = jnp.where(blk == 0, 0, jnp.where(blk == last_blk, 2, 1))
        kw = k_ref[0, pl.ds(k0, nk), :]
        vw = v_ref[0, pl.ds(k0, nk), :]
        q = q_ref[0, u * mq:(u + 1) * mq, :]
        o, _, l = _pair_attention(q, kw, vw, bias_ref[variant, 0])
        o_ref[0, u * mq:(u + 1) * mq, :] = (o / l).astype(_BF16)


def _natten_bias(rpb):
    n_off = 2 * WIN_ROWS - 1
    toep = _toeplitz(rpb.astype(_F32) * LOG2E, GRID_W, GRID_W, WIN_COLS - 1)
    c = np.arange(GRID_W)[:, None]
    kc = np.arange(GRID_W)[None, :]
    col_start = np.clip(c - WIN_COLS // 2, 0, GRID_W - WIN_COLS)
    col_ok = (kc >= col_start) & (kc < col_start + WIN_COLS)
    col_bias = jnp.where(jnp.asarray(col_ok), toep, NEG)
    neg_block = jnp.full((H_A, GRID_W, GRID_W), NEG, _F32)
    fake_rows = 3 * KROWS_A
    nblk = fake_rows // QROWS_A
    variants = []
    for blk in (0, nblk // 2, nblk - 1):
        krow0 = int(np.clip(QROWS_A * blk - WIN_ROWS // 2, 0, fake_rows - KROWS_A))
        q_rows = []
        for rq in range(QROWS_A):
            r = QROWS_A * blk + rq
            start = int(np.clip(r - WIN_ROWS // 2, 0, fake_rows - WIN_ROWS))
            blocks = []
            for rk in range(KROWS_A):
                kr = krow0 + rk
                row_off = kr - r + WIN_ROWS - 1
                inside = start <= kr < start + WIN_ROWS
                assert not inside or 0 <= row_off < n_off
                blocks.append(col_bias[:, row_off] if inside else neg_block)
            q_rows.append(jnp.concatenate(blocks, axis=-1))
        variants.append(jnp.concatenate(q_rows, axis=-2))
    tiles = jnp.stack(variants)
    return tiles.reshape(3, H_A // 2, 2 * QROWS_A * GRID_W, KROWS_A * GRID_W)


def _natten_call(qa, ka, va, bias):
    B, T, _ = qa.shape
    rows = T // GRID_W
    rows_per_step = QROWS_A * min(BLOCKS_PER_STEP_A, rows // QROWS_A)
    assert T % GRID_W == 0 and rows % rows_per_step == 0 and rows >= KROWS_A
    mq, nk = rows_per_step * GRID_W, KROWS_A * GRID_W
    return pl.pallas_call(
        functools.partial(_natten_kernel, rows=rows),
        grid=(B, H_A // 2, rows // rows_per_step),
        in_specs=[pl.BlockSpec((1, mq, LANES), lambda b, p, i: (b, i, p)),
                  pl.BlockSpec((1, T, LANES), lambda b, p, i: (b, 0, p)),
                  pl.BlockSpec((1, T, LANES), lambda b, p, i: (b, 0, p)),
                  pl.BlockSpec((3, 1, 2 * QROWS_A * GRID_W, nk), lambda b, p, i: (0, p, 0, 0))],
        out_specs=pl.BlockSpec((1, mq, LANES), lambda b, p, i: (b, i, p)),
        out_shape=jax.ShapeDtypeStruct((B, T, QA_W), _BF16),
        compiler_params=pltpu.CompilerParams(
            dimension_semantics=("arbitrary", "arbitrary", "arbitrary"),
            vmem_limit_bytes=VMEM_LIMIT),
        name="natten",
    )(qa, ka, va, bias)


def _dilated_kernel(q0_ref, k0_ref, v0_ref, q1_ref, k1_ref, v1_ref, q2_ref, k2_ref, v2_ref,
                    bias_ref, o_ref, so_ref, sm_ref, sl_ref, stage_ref):
    j = pl.program_id(2)
    chunk = o_ref.shape[1]
    refs = ((q0_ref, k0_ref, v0_ref), (q1_ref, k1_ref, v1_ref), (q2_ref, k2_ref, v2_ref))

    def tile(g, r, sb):
        dil = DILATIONS_B[g]
        q_ref, k_ref, v_ref = refs[g]
        sub_len = k_ref.shape[2]
        lq0 = j * (chunk // dil) + sb * QBLK_B
        kl0 = jnp.clip(lq0 - HALF_KEYS, 0, sub_len - KBLK_B)
        variant = (lq0 - kl0) // HALF_KEYS
        kl0 = pl.multiple_of(kl0, HALF_KEYS)
        q = q_ref[0, r, sb * QBLK_B:(sb + 1) * QBLK_B, :]
        kw = k_ref[0, r, pl.ds(kl0, KBLK_B), :]
        vw = v_ref[0, r, pl.ds(kl0, KBLK_B), :]
        res = _pair_attention(q, kw, vw, bias_ref[0, g, variant])
        if dil > DEINT_STEP:
            lo, hi = r % DEINT_STEP, r // DEINT_STEP
            rows_out = pl.ds(sb * QBLK_B * DEINT_STEP + hi, QBLK_B, stride=DEINT_STEP)
            for a, val in enumerate(res):
                stage_ref[a, lo, rows_out, :] = val
            return
        if dil == 1:
            rows_out = pl.ds(sb * QBLK_B, QBLK_B)
        else:
            rows_out = pl.ds(sb * QBLK_B * dil + r, QBLK_B, stride=dil)
        for dst, val in zip((so_ref, sm_ref, sl_ref), res):
            dst[g, rows_out, :] = val

    def flush_stage(g, lo):
        rows_out = pl.ds(lo, chunk // DEINT_STEP, stride=DEINT_STEP)
        for a, dst in enumerate((so_ref, sm_ref, sl_ref)):
            dst[g, rows_out, :] = stage_ref[a, lo]

    def combine(rows):
        m_all = jnp.maximum(jnp.maximum(sm_ref[0, rows, :], sm_ref[1, rows, :]), sm_ref[2, rows, :])
        num = den = None
        for g in range(N_GROUPS_B):
            w = jnp.exp2(sm_ref[g, rows, :] - m_all)
            num = w * so_ref[g, rows, :] if num is None else num + w * so_ref[g, rows, :]
            den = w * sl_ref[g, rows, :] if den is None else den + w * sl_ref[g, rows, :]
        o_ref[0, rows, :] = (num / den).astype(_BF16)

    for g in reversed(range(N_GROUPS_B)):
        dil = DILATIONS_B[g]
        nsb = chunk // dil // QBLK_B
        if dil > DEINT_STEP:
            assert dil == DEINT_STEP * DEINT_STEP
            for lo in range(DEINT_STEP):
                for hi in range(DEINT_STEP):
                    for sb in range(nsb):
                        tile(g, hi * DEINT_STEP + lo, sb)
                flush_stage(g, lo)
            continue
        for r in range(dil):
            for sb in range(nsb):
                tile(g, r, sb)
                if dil == 1:
                    combine(slice(sb * QBLK_B, (sb + 1) * QBLK_B))


def _t5_buckets(rel):
    half = NUM_BUCKETS // 2
    ret = np.where(rel > 0, half, 0)
    n = np.abs(rel)
    max_exact = half // 2
    large = max_exact + (np.log(np.maximum(n, 1) / max_exact)
                         / np.log(T5_MAX_DIST / max_exact) * (half - max_exact)).astype(np.int32)
    large = np.minimum(large, half - 1)
    return (ret + np.where(n < max_exact, n, large)).astype(np.int32)


def _dilated_bias(t5_bias):
    rel = np.arange(-HALF_KEYS, HALF_KEYS + 1)
    tab = t5_bias.astype(_F32).T * LOG2E
    by_rel = jnp.stack([
        jnp.take(tab[g * HG_B:(g + 1) * HG_B], jnp.asarray(_t5_buckets(rel * dil)), axis=1)
        for g, dil in enumerate(DILATIONS_B)])
    tiles = jnp.stack([
        _toeplitz(by_rel, QBLK_B, KBLK_B, HALF_KEYS - variant * HALF_KEYS)
        for variant in range(3)])
    tiles = tiles.reshape(3, N_GROUPS_B, HG_B // 2, 2 * QBLK_B, KBLK_B)
    return jnp.transpose(tiles, (2, 1, 0, 3, 4))


def _dilated_call(b0, b1, b2, bias):
    B = b0.shape[0]
    T = b0.shape[2]
    chunk = min(CHUNK_B, T)
    assert T % chunk == 0 and chunk % (QBLK_B * max(DILATIONS_B)) == 0
    assert T // max(DILATIONS_B) >= KBLK_B and DILATIONS_B[0] == 1
    in_specs, args = [], []
    for arr, dil in zip((b0, b1, b2), DILATIONS_B):
        sub_len = T // dil
        in_specs += [
            pl.BlockSpec((1, dil, chunk // dil, LANES), lambda b, p, j: (b, 0, j, p)),
            pl.BlockSpec((1, dil, sub_len, LANES), lambda b, p, j: (b, 0, 0, 2 + p)),
            pl.BlockSpec((1, dil, sub_len, LANES), lambda b, p, j: (b, 0, 0, 4 + p)),
        ]
        args += [arr, arr, arr]
    in_specs.append(pl.BlockSpec((1, N_GROUPS_B, 3, 2 * QBLK_B, KBLK_B),
                                 lambda b, p, j: (p, 0, 0, 0, 0)))
    args.append(bias)
    return pl.pallas_call(
        _dilated_kernel,
        grid=(B, HG_B // 2, T // chunk),
        in_specs=in_specs,
        out_specs=pl.BlockSpec((1, chunk, LANES), lambda b, p, j: (b, j, p)),
        out_shape=jax.ShapeDtypeStruct((B, T, GB_W), _BF16),
        scratch_shapes=[pltpu.VMEM((N_GROUPS_B, chunk, LANES), _F32)] * 3
        + [pltpu.VMEM((3, DEINT_STEP, chunk // DEINT_STEP, LANES), _F32)],
        compiler_params=pltpu.CompilerParams(
            dimension_semantics=("arbitrary", "arbitrary", "arbitrary"),
            vmem_limit_bytes=VMEM_LIMIT),
        name="dilated",
    )(*args)


def _mlp_kernel(x_ref, oa_ref, ob_ref, gate_ref, wa_ref, wb_ref, wo_ref, gmlp_ref, wup_ref,
                wdn_ref, y_ref):
    ya = jnp.dot(oa_ref[...], wa_ref[...], preferred_element_type=_F32)
    yb = jnp.dot(ob_ref[...], wb_ref[...], preferred_element_type=_F32)
    ga = gate_ref[:, :D_MODEL].astype(_F32)
    gb = gate_ref[:, D_MODEL:].astype(_F32)
    merged = (ga * ya + gb * yb).astype(_BF16)
    x1 = x_ref[...] + jnp.dot(merged, wo_ref[...], preferred_element_type=_F32)
    ms = jnp.mean(x1 * x1, axis=-1, keepdims=True)
    hm = (x1 * lax.rsqrt(ms + EPS) * gmlp_ref[...]).astype(_BF16)
    acc = x1
    for c in range(D_FF // FF_CHUNK):
        sl = slice(c * FF_CHUNK, (c + 1) * FF_CHUNK)
        u = jnp.maximum(jnp.dot(hm, wup_ref[:, sl], preferred_element_type=_F32), 0.0)
        acc = acc + jnp.dot((u * u).astype(_BF16), wdn_ref[sl, :], preferred_element_type=_F32)
    y_ref[...] = acc


def _mlp_call(x2, oa2, ob2, gates2, wa, wb, wo, gmlp, wup, wdn):
    n_tok, D = x2.shape
    tm = TM_MLP
    assert n_tok % tm == 0
    tok = lambda w: pl.BlockSpec((tm, w), lambda i: (i, 0))
    return pl.pallas_call(
        _mlp_kernel,
        grid=(n_tok // tm,),
        in_specs=[tok(D), tok(QA_W), tok(GB_W), tok(2 * D_MODEL),
                  _resident((QA_W, D)), _resident((GB_W, D)), _resident((D, D)),
                  _resident((1, D)), _resident((D, D_FF)), _resident((D_FF, D))],
        out_specs=tok(D),
        out_shape=jax.ShapeDtypeStruct((n_tok, D), _F32),
        compiler_params=pltpu.CompilerParams(
            dimension_semantics=("arbitrary",), vmem_limit_bytes=VMEM_LIMIT_DENSE),
        name="mlp",
    )(x2, oa2, ob2, gates2, wa, wb, wo, gmlp, wup, wdn)


def _encoder_layer(x, p):
    B, T, D = x.shape
    qa, ka, va, b0, b1, b2, gates = _proj_call(
        x, p["gmix"], p["w_in"], p["gqa"], p["gka"], p["gqb"], p["gkb"])
    oa = _natten_call(qa, ka, va, p["bias_a"])
    ob = _dilated_call(b0, b1, b2, p["bias_b"])
    y = _mlp_call(x.reshape(B * T, D), oa.reshape(B * T, QA_W), ob.reshape(B * T, GB_W),
                  gates.reshape(B * T, 2 * D_MODEL), p["wa"], p["wb"], p["wo"], p["gmlp"],
                  p["wup"], p["wdn"])
    return y.reshape(B, T, D)


def _layer_params(norm_mix, w_in, q_norm_a, k_norm_a, q_norm_b, k_norm_b, rpb_a, t5_bias,
                  w_branch_a, w_branch_b, w_out, norm_mlp, w_up, w_down):
    tile_gain = lambda g: jnp.tile(g.astype(_F32), NORM_CHUNK // HEAD_DIM).reshape(1, NORM_CHUNK)
    return dict(
        gmix=norm_mix.astype(_F32).reshape(1, D_MODEL), w_in=w_in.astype(_BF16),
        gqa=tile_gain(q_norm_a), gka=tile_gain(k_norm_a),
        gqb=tile_gain(q_norm_b), gkb=tile_gain(k_norm_b),
        bias_a=_natten_bias(rpb_a), bias_b=_dilated_bias(t5_bias),
        wa=w_branch_a.astype(_BF16), wb=w_branch_b.astype(_BF16), wo=w_out.astype(_BF16),
        gmlp=norm_mlp.astype(_F32).reshape(1, D_MODEL),
        wup=w_up.astype(_BF16), wdn=w_down.astype(_BF16))


def kernel(x_prompt, x_sample, norm_mix, w_in, q_norm_a, k_norm_a, q_norm_b, k_norm_b, rpb_a,
           t5_bias, w_branch_a, w_branch_b, w_out, norm_mlp, w_up, w_down):
    y_prompt, y_sample = x_prompt, x_sample
    for l in range(norm_mix.shape[0]):
        p = _layer_params(norm_mix[l], w_in[l], q_norm_a[l], k_norm_a[l], q_norm_b[l],
                          k_norm_b[l], rpb_a[l], t5_bias, w_branch_a[l], w_branch_b[l],
                          w_out[l], norm_mlp[l], w_up[l], w_down[l])
        y_prompt = _encoder_layer(y_prompt, p)
        y_sample = _encoder_layer(y_sample, p)
    return (y_prompt, y_sample)
```

```python
import functools
import math

import numpy as np
import jax
import jax.numpy as jnp
from jax import lax
from jax.experimental import pallas as pl
from jax.experimental.pallas import tpu as pltpu

D_MODEL = 1024
HEAD_DIM = 64
H_A = 8
N_GROUPS_B = 3
HG_B = 4
H_B = N_GROUPS_B * HG_B
WINDOWS_B = (128, 512, 2048)
DILATIONS_B = (1, 4, 16)
GRID_W = 64
WIN_ROWS = 8
WIN_COLS = 16
NUM_BUCKETS = 32
T5_MAX_DIST = 1024
D_FF = 4 * D_MODEL
EPS = 1e-6
NEG = -1e30
LOG2E = math.log2(math.e)
QA_W = H_A * HEAD_DIM
QB_W = H_B * HEAD_DIM
GB_W = HG_B * HEAD_DIM
IN_W = 3 * QA_W + 3 * QB_W + 2 * D_MODEL
HALF_KEYS = 64
assert all((w // 2) // d == HALF_KEYS for w, d in zip(WINDOWS_B, DILATIONS_B))

LANES = 128
NORM_CHUNK = 256
TM_PROJ = 1024
DEINT_STEP = 4
PROJ_BLOCK = 1024
TM_MLP = 1024
FF_CHUNK = 1024
QROWS_A = 4
KROWS_A = QROWS_A + 8
BLOCKS_PER_STEP_A = 16
QBLK_B = 128
KBLK_B = QBLK_B + 2 * HALF_KEYS
CHUNK_B = 2048
VMEM_BYTES_V7X = 64 * 1024 * 1024
VMEM_LIMIT = VMEM_BYTES_V7X - 8 * 1024 * 1024
VMEM_LIMIT_DENSE = VMEM_BYTES_V7X - 4 * 1024 * 1024

_F32 = jnp.float32
_BF16 = jnp.bfloat16


def _resident(shape):
    nd = len(shape)
    return pl.BlockSpec(shape, lambda *_: (0,) * nd, pipeline_mode=pl.Buffered(1))


def _proj_kernel(x_ref, gmix_ref, w_ref, gqa_ref, gka_ref, gqb_ref, gkb_ref,
                 qa_ref, ka_ref, va_ref, b0_ref, b1_ref, b2_ref, gate_ref, scr_ref, scr2_ref):
    tm = x_ref.shape[1]
    x = x_ref[0]
    ms = jnp.mean(x * x, axis=-1, keepdims=True)
    h = (x * lax.rsqrt(ms + EPS) * gmix_ref[...]).astype(_BF16)

    wide = {}

    def chunk(c0):
        b0 = (c0 // PROJ_BLOCK) * PROJ_BLOCK
        if b0 not in wide:
            b1 = min(b0 + PROJ_BLOCK, IN_W)
            wide[b0] = jnp.dot(h, w_ref[:, b0:b1], preferred_element_type=_F32)
        return wide[b0][:, c0 - b0:c0 - b0 + NORM_CHUNK]

    lane = lax.broadcasted_iota(jnp.int32, (tm, LANES), 1)
    first = lane < HEAD_DIM

    def head_norm(t, gain_ref, scale):
        parts = []
        for c in range(t.shape[1] // LANES):
            tc = t[:, c * LANES:(c + 1) * LANES]
            sq = tc * tc
            s0 = jnp.sum(jnp.where(first, sq, 0.0), axis=-1, keepdims=True)
            s1 = jnp.sum(jnp.where(first, 0.0, sq), axis=-1, keepdims=True)
            msq = jnp.where(first, s0, s1) * (1.0 / HEAD_DIM)
            parts.append(tc * lax.rsqrt(msq + EPS))
        return jnp.concatenate(parts, axis=1) * (gain_ref[...] * scale)

    qk_scale = HEAD_DIM ** -0.5 * LOG2E

    for c in range(QA_W // NORM_CHUNK):
        sl = slice(c * NORM_CHUNK, (c + 1) * NORM_CHUNK)
        qa_ref[0, :, sl] = head_norm(chunk(sl.start), gqa_ref, qk_scale).astype(_BF16)
        ka_ref[0, :, sl] = head_norm(chunk(QA_W + sl.start), gka_ref, 1.0).astype(_BF16)
        va_ref[0, :, sl] = chunk(2 * QA_W + sl.start).astype(_BF16)

    outs = (b0_ref, b1_ref, b2_ref)
    base = 3 * QA_W
    slab = slab2 = 0
    for kind in range(3):
        for g, dil in enumerate(DILATIONS_B):
            tg = chunk(base + kind * QB_W + g * GB_W)
            if kind == 0:
                tg = head_norm(tg, gqb_ref, qk_scale)
            elif kind == 1:
                tg = head_norm(tg, gkb_ref, 1.0)
            lane0 = kind * GB_W
            if dil == 1:
                outs[g][0, 0, :, lane0:lane0 + GB_W] = tg.astype(_BF16)
                continue
            for half in range(GB_W // LANES):
                l0 = lane0 + half * LANES
                scr_ref[slab] = tg[:, half * LANES:(half + 1) * LANES]
                if dil == DEINT_STEP:
                    for r in range(dil):
                        sub = scr_ref[slab, pl.ds(r, tm // dil, stride=dil), :]
                        outs[g][0, r, :, l0:l0 + LANES] = sub.astype(_BF16)
                else:
                    assert dil == DEINT_STEP * DEINT_STEP
                    for lo in range(DEINT_STEP):
                        scr2_ref[slab2, lo] = scr_ref[
                            slab, pl.ds(lo, tm // DEINT_STEP, stride=DEINT_STEP), :]
                    for lo in range(DEINT_STEP):
                        for hi in range(DEINT_STEP):
                            sub = scr2_ref[slab2, lo, pl.ds(hi, tm // dil, stride=DEINT_STEP), :]
                            outs[g][0, hi * DEINT_STEP + lo, :, l0:l0 + LANES] = sub.astype(_BF16)
                    slab2 += 1
                slab += 1

    base = 3 * QA_W + 3 * QB_W
    for c in range(2 * D_MODEL // NORM_CHUNK):
        sl = slice(c * NORM_CHUNK, (c + 1) * NORM_CHUNK)
        gate = 0.5 * jnp.tanh(0.5 * chunk(base + sl.start)) + 0.5
        gate_ref[0, :, sl] = gate.astype(_BF16)


def _proj_call(x, gmix, w_in, gqa, gka, gqb, gkb):
    B, T, D = x.shape
    tm = TM_PROJ
    assert T % tm == 0 and tm % (16 * max(DILATIONS_B)) == 0
    n_slabs = 3 * sum(1 for d in DILATIONS_B if d > 1) * (GB_W // LANES)
    n_slabs2 = 3 * sum(1 for d in DILATIONS_B if d > DEINT_STEP) * (GB_W // LANES)
    tok = lambda w: pl.BlockSpec((1, tm, w), lambda b, i: (b, i, 0))
    sub = lambda d: pl.BlockSpec((1, d, tm // d, 3 * GB_W), lambda b, i: (b, 0, i, 0))
    out_shape = (
        jax.ShapeDtypeStruct((B, T, QA_W), _BF16),
        jax.ShapeDtypeStruct((B, T, QA_W), _BF16),
        jax.ShapeDtypeStruct((B, T, QA_W), _BF16),
    ) + tuple(jax.ShapeDtypeStruct((B, d, T // d, 3 * GB_W), _BF16) for d in DILATIONS_B) + (
        jax.ShapeDtypeStruct((B, T, 2 * D_MODEL), _BF16),
    )
    return pl.pallas_call(
        _proj_kernel,
        grid=(B, T // tm),
        in_specs=[tok(D), _resident((1, D)), _resident((D, IN_W)),
                  _resident((1, NORM_CHUNK)), _resident((1, NORM_CHUNK)),
                  _resident((1, NORM_CHUNK)), _resident((1, NORM_CHUNK))],
        out_specs=(tok(QA_W), tok(QA_W), tok(QA_W)) + tuple(sub(d) for d in DILATIONS_B)
        + (tok(2 * D_MODEL),),
        out_shape=out_shape,
        scratch_shapes=[pltpu.VMEM((n_slabs, tm, LANES), _F32),
                        pltpu.VMEM((n_slabs2, DEINT_STEP, tm // DEINT_STEP, LANES), _F32)],
        compiler_params=pltpu.CompilerParams(
            dimension_semantics=("arbitrary", "arbitrary"), vmem_limit_bytes=VMEM_LIMIT_DENSE),
        name="proj",
    )(x, gmix, w_in, gqa, gka, gqb, gkb)


def _pair_attention(q, kw, vw, bias2):
    m_rows = q.shape[0]
    lane = lax.broadcasted_iota(jnp.int32, (m_rows, LANES), 1)
    first = lane < HEAD_DIM
    zero = jnp.zeros_like(q)
    q2 = jnp.concatenate([jnp.where(first, q, zero), jnp.where(first, zero, q)], axis=0)
    s = lax.dot_general(q2, kw, (((1,), (1,)), ((), ())), preferred_element_type=_F32)
    s = s + bias2.astype(_F32)
    m = jnp.max(s, axis=-1, keepdims=True)
    p = jnp.exp2(s - m)
    o2l = jnp.dot(p.astype(_BF16), jnp.concatenate([vw, jnp.ones_like(vw)], axis=1),
                  preferred_element_type=_F32)
    o2, l = o2l[:, :LANES], o2l[:, LANES:]

    def pick(x2):
        x2 = jnp.broadcast_to(x2, (2 * m_rows, LANES))
        return jnp.where(first, x2[:m_rows], x2[m_rows:])

    return pick(o2), pick(m), pick(l)


def _toeplitz(v, n_rows, n_cols, center):
    n = v.shape[-1]
    period = n_rows + n_cols - 1
    left = n_rows - 1 - center
    cfg = [(0, 0, 0)] * (v.ndim - 1) + [(left, period - n - left, 0)]
    ext = lax.pad(v, jnp.asarray(NEG, v.dtype), cfg)
    w = jnp.concatenate([ext[..., n_rows - 1:], ext[..., :n_rows - 1]], axis=-1)
    flat = jnp.tile(w, (1,) * (v.ndim - 1) + (n_rows,))[..., :n_rows * (period - 1)]
    return flat.reshape(v.shape[:-1] + (n_rows, period - 1))[..., :n_cols]


def _natten_kernel(q_ref, k_ref, v_ref, bias_ref, o_ref, *, rows):
    i = pl.program_id(2)
    mq, nk = QROWS_A * GRID_W, KROWS_A * GRID_W
    last_blk = rows // QROWS_A - 1
    blocks_per_step = q_ref.shape[1] // mq
    for u in range(blocks_per_step):
        blk = blocks_per_step * i + u
        krow0 = jnp.clip(QROWS_A * blk - WIN_ROWS // 2, 0, rows - KROWS_A)
        k0 = pl.multiple_of(krow0 * GRID_W, GRID_W)
        variant = jnp.where(blk == 0, 0, jnp.where(blk == last_blk, 2, 1))
        kw = k_ref[0, pl.ds(k0, nk), :]
        vw = v_ref[0, pl.ds(k0, nk), :]
        q = q_ref[0, u * mq:(u + 1) * mq, :]
        o, _, l = _pair_attention(q, kw, vw, bias_ref[variant, 0])
        o_ref[0, u * mq:(u + 1) * mq, :] = (o / l).astype(_BF16)


def _natten_bias(rpb):
    n_off = 2 * WIN_ROWS - 1
    toep = _toeplitz(rpb.astype(_F32) * LOG2E, GRID_W, GRID_W, WIN_COLS - 1)
    c = np.arange(GRID_W)[:, None]
    kc = np.arange(GRID_W)[None, :]
    col_start = np.clip(c - WIN_COLS // 2, 0, GRID_W - WIN_COLS)
    col_ok = (kc >= col_start) & (kc < col_start + WIN_COLS)
    col_bias = jnp.where(jnp.asarray(col_ok), toep, NEG)
    neg_block = jnp.full((H_A, GRID_W, GRID_W), NEG, _F32)
    fake_rows = 3 * KROWS_A
    nblk = fake_rows // QROWS_A
    variants = []
    for blk in (0, nblk // 2, nblk - 1):
        krow0 = int(np.clip(QROWS_A * blk - WIN_ROWS // 2, 0, fake_rows - KROWS_A))
        q_rows = []
        for rq in range(QROWS_A):
            r = QROWS_A * blk + rq
            start = int(np.clip(r - WIN_ROWS // 2, 0, fake_rows - WIN_ROWS))
            blocks = []
            for rk in range(KROWS_A):
                kr = krow0 + rk
                row_off = kr - r + WIN_ROWS - 1
                inside = start <= kr < start + WIN_ROWS
                assert not inside or 0 <= row_off < n_off
                blocks.append(col_bias[:, row_off] if inside else neg_block)
            q_rows.append(jnp.concatenate(blocks, axis=-1))
        variants.append(jnp.concatenate(q_rows, axis=-2))
    tiles = jnp.stack(variants)
    return tiles.reshape(3, H_A // 2, 2 * QROWS_A * GRID_W, KROWS_A * GRID_W)


def _natten_call(qa, ka, va, bias):
    B, T, _ = qa.shape
    rows = T // GRID_W
    rows_per_step = QROWS_A * min(BLOCKS_PER_STEP_A, rows // QROWS_A)
    assert T % GRID_W == 0 and rows % rows_per_step == 0 and rows >= KROWS_A
    mq, nk = rows_per_step * GRID_W, KROWS_A * GRID_W
    return pl.pallas_call(
        functools.partial(_natten_kernel, rows=rows),
        grid=(B, H_A // 2, rows // rows_per_step),
        in_specs=[pl.BlockSpec((1, mq, LANES), lambda b, p, i: (b, i, p)),
                  pl.BlockSpec((1, T, LANES), lambda b, p, i: (b, 0, p)),
                  pl.BlockSpec((1, T, LANES), lambda b, p, i: (b, 0, p)),
                  pl.BlockSpec((3, 1, 2 * QROWS_A * GRID_W, nk), lambda b, p, i: (0, p, 0, 0))],
        out_specs=pl.BlockSpec((1, mq, LANES), lambda b, p, i: (b, i, p)),
        out_shape=jax.ShapeDtypeStruct((B, T, QA_W), _BF16),
        compiler_params=pltpu.CompilerParams(
            dimension_semantics=("arbitrary", "arbitrary", "arbitrary"),
            vmem_limit_bytes=VMEM_LIMIT),
        name="natten",
    )(qa, ka, va, bias)


def _dilated_kernel(q0_ref, k0_ref, v0_ref, q1_ref, k1_ref, v1_ref, q2_ref, k2_ref, v2_ref,
                    bias_ref, o_ref, so_ref, sm_ref, sl_ref, stage_ref):
    j = pl.program_id(2)
    chunk = o_ref.shape[1]
    refs = ((q0_ref, k0_ref, v0_ref), (q1_ref, k1_ref, v1_ref), (q2_ref, k2_ref, v2_ref))

    def tile(g, r, sb):
        dil = DILATIONS_B[g]
        q_ref, k_ref, v_ref = refs[g]
        sub_len = k_ref.shape[2]
        lq0 = j * (chunk // dil) + sb * QBLK_B
        kl0 = jnp.clip(lq0 - HALF_KEYS, 0, sub_len - KBLK_B)
        variant = (lq0 - kl0) // HALF_KEYS
        kl0 = pl.multiple_of(kl0, HALF_KEYS)
        q = q_ref[0, r, sb * QBLK_B:(sb + 1) * QBLK_B, :]
        kw = k_ref[0, r, pl.ds(kl0, KBLK_B), :]
        vw = v_ref[0, r, pl.ds(kl0, KBLK_B), :]
        res = _pair_attention(q, kw, vw, bias_ref[0, g, variant])
        if dil > DEINT_STEP:
            lo, hi = r % DEINT_STEP, r // DEINT_STEP
            rows_out = pl.ds(sb * QBLK_B * DEINT_STEP + hi, QBLK_B, stride=DEINT_STEP)
            for a, val in enumerate(res):
                stage_ref[a, lo, rows_out, :] = val
            return
        if dil == 1:
            rows_out = pl.ds(sb * QBLK_B, QBLK_B)
        else:
            rows_out = pl.ds(sb * QBLK_B * dil + r, QBLK_B, stride=dil)
        for dst, val in zip((so_ref, sm_ref, sl_ref), res):
            dst[g, rows_out, :] = val

    def flush_stage(g, lo):
        rows_out = pl.ds(lo, chunk // DEINT_STEP, stride=DEINT_STEP)
        for a, dst in enumerate((so_ref, sm_ref, sl_ref)):
            dst[g, rows_out, :] = stage_ref[a, lo]

    def combine(rows):
        m_all = jnp.maximum(jnp.maximum(sm_ref[0, rows, :], sm_ref[1, rows, :]), sm_ref[2, rows, :])
        num = den = None
        for g in range(N_GROUPS_B):
            w = jnp.exp2(sm_ref[g, rows, :] - m_all)
            num = w * so_ref[g, rows, :] if num is None else num + w * so_ref[g, rows, :]
            den = w * sl_ref[g, rows, :] if den is None else den + w * sl_ref[g, rows, :]
        o_ref[0, rows, :] = (num / den).astype(_BF16)

    for g in reversed(range(N_GROUPS_B)):
        dil = DILATIONS_B[g]
        nsb = chunk // dil // QBLK_B
        if dil > DEINT_STEP:
            assert dil == DEINT_STEP * DEINT_STEP
            for lo in range(DEINT_STEP):
                for hi in range(DEINT_STEP):
                    for sb in range(nsb):
                        tile(g, hi * DEINT_STEP + lo, sb)
                flush_stage(g, lo)
            continue
        for r in range(dil):
            for sb in range(nsb):
                tile(g, r, sb)
                if dil == 1:
                    combine(slice(sb * QBLK_B, (sb + 1) * QBLK_B))


def _t5_buckets(rel):
    half = NUM_BUCKETS // 2
    ret = np.where(rel > 0, half, 0)
    n = np.abs(rel)
    max_exact = half // 2
    large = max_exact + (np.log(np.maximum(n, 1) / max_exact)
                         / np.log(T5_MAX_DIST / max_exact) * (half - max_exact)).astype(np.int32)
    large = np.minimum(large, half - 1)
    return (ret + np.where(n < max_exact, n, large)).astype(np.int32)


def _dilated_bias(t5_bias):
    rel = np.arange(-HALF_KEYS, HALF_KEYS + 1)
    tab = t5_bias.astype(_F32).T * LOG2E
    by_rel = jnp.stack([
        jnp.take(tab[g * HG_B:(g + 1) * HG_B], jnp.asarray(_t5_buckets(rel * dil)), axis=1)
        for g, dil in enumerate(DILATIONS_B)])
    tiles = jnp.stack([
        _toeplitz(by_rel, QBLK_B, KBLK_B, HALF_KEYS - variant * HALF_KEYS)
        for variant in range(3)])
    tiles = tiles.reshape(3, N_GROUPS_B, HG_B // 2, 2 * QBLK_B, KBLK_B)
    return jnp.transpose(tiles, (2, 1, 0, 3, 4))


def _dilated_call(b0, b1, b2, bias):
    B = b0.shape[0]
    T = b0.shape[2]
    chunk = min(CHUNK_B, T)
    assert T % chunk == 0 and chunk % (QBLK_B * max(DILATIONS_B)) == 0
    assert T // max(DILATIONS_B) >= KBLK_B and DILATIONS_B[0] == 1
    in_specs, args = [], []
    for arr, dil in zip((b0, b1, b2), DILATIONS_B):
        sub_len = T // dil
        in_specs += [
            pl.BlockSpec((1, dil, chunk // dil, LANES), lambda b, p, j: (b, 0, j, p)),
            pl.BlockSpec((1, dil, sub_len, LANES), lambda b, p, j: (b, 0, 0, 2 + p)),
            pl.BlockSpec((1, dil, sub_len, LANES), lambda b, p, j: (b, 0, 0, 4 + p)),
        ]
        args += [arr, arr, arr]
    in_specs.append(pl.BlockSpec((1, N_GROUPS_B, 3, 2 * QBLK_B, KBLK_B),
                                 lambda b, p, j: (p, 0, 0, 0, 0)))
    args.append(bias)
    return pl.pallas_call(
        _dilated_kernel,
        grid=(B, HG_B // 2, T // chunk),
        in_specs=in_specs,
        out_specs=pl.BlockSpec((1, chunk, LANES), lambda b, p, j: (b, j, p)),
        out_shape=jax.ShapeDtypeStruct((B, T, GB_W), _BF16),
        scratch_shapes=[pltpu.VMEM((N_GROUPS_B, chunk, LANES), _F32)] * 3
        + [pltpu.VMEM((3, DEINT_STEP, chunk // DEINT_STEP, LANES), _F32)],
        compiler_params=pltpu.CompilerParams(
            dimension_semantics=("arbitrary", "arbitrary", "arbitrary"),
            vmem_limit_bytes=VMEM_LIMIT),
        name="dilated",
    )(*args)


def _mlp_kernel(x_ref, oa_ref, ob_ref, gate_ref, wa_ref, wb_ref, wo_ref, gmlp_ref, wup_ref,
                wdn_ref, y_ref):
    ya = jnp.dot(oa_ref[...], wa_ref[...], preferred_element_type=_F32)
    yb = jnp.dot(ob_ref[...], wb_ref[...], preferred_element_type=_F32)
    ga = gate_ref[:, :D_MODEL].astype(_F32)
    gb = gate_ref[:, D_MODEL:].astype(_F32)
    merged = (ga * ya + gb * yb).astype(_BF16)
    x1 = x_ref[...] + jnp.dot(merged, wo_ref[...], preferred_element_type=_F32)
    ms = jnp.mean(x1 * x1, axis=-1, keepdims=True)
    hm = (x1 * lax.rsqrt(ms + EPS) * gmlp_ref[...]).astype(_BF16)
    acc = x1
    for c in range(D_FF // FF_CHUNK):
        sl = slice(c * FF_CHUNK, (c + 1) * FF_CHUNK)
        u = jnp.maximum(jnp.dot(hm, wup_ref[:, sl], preferred_element_type=_F32), 0.0)
        acc = acc + jnp.dot((u * u).astype(_BF16), wdn_ref[sl, :], preferred_element_type=_F32)
    y_ref[...] = acc


def _mlp_call(x2, oa2, ob2, gates2, wa, wb, wo, gmlp, wup, wdn):
    n_tok, D = x2.shape
    tm = TM_MLP
    assert n_tok % tm == 0
    tok = lambda w: pl.BlockSpec((tm, w), lambda i: (i, 0))
    return pl.pallas_call(
        _mlp_kernel,
        grid=(n_tok // tm,),
        in_specs=[tok(D), tok(QA_W), tok(GB_W), tok(2 * D_MODEL),
                  _resident((QA_W, D)), _resident((GB_W, D)), _resident((D, D)),
                  _resident((1, D)), _resident((D, D_FF)), _resident((D_FF, D))],
        out_specs=tok(D),
        out_shape=jax.ShapeDtypeStruct((n_tok, D), _F32),
        compiler_params=pltpu.CompilerParams(
            dimension_semantics=("arbitrary",), vmem_limit_bytes=VMEM_LIMIT_DENSE),
        name="mlp",
    )(x2, oa2, ob2, gates2, wa, wb, wo, gmlp, wup, wdn)


def _encoder_layer(x, p):
    B, T, D = x.shape
    qa, ka, va, b0, b1, b2, gates = _proj_call(
        x, p["gmix"], p["w_in"], p["gqa"], p["gka"], p["gqb"], p["gkb"])
    oa = _natten_call(qa, ka, va, p["bias_a"])
    ob = _dilated_call(b0, b1, b2, p["bias_b"])
    y = _mlp_call(x.reshape(B * T, D), oa.reshape(B * T, QA_W), ob.reshape(B * T, GB_W),
                  gates.reshape(B * T, 2 * D_MODEL), p["wa"], p["wb"], p["wo"], p["gmlp"],
                  p["wup"], p["wdn"])
    return y.reshape(B, T, D)


def _layer_params(norm_mix, w_in, q_norm_a, k_norm_a, q_norm_b, k_norm_b, rpb_a, t5_bias,
                  w_branch_a, w_branch_b, w_out, norm_mlp, w_up, w_down):
    tile_gain = lambda g: jnp.tile(g.astype(_F32), NORM_CHUNK // HEAD_DIM).reshape(1, NORM_CHUNK)
    return dict(
        gmix=norm_mix.astype(_F32).reshape(1, D_MODEL), w_in=w_in.astype(_BF16),
        gqa=tile_gain(q_norm_a), gka=tile_gain(k_norm_a),
        gqb=tile_gain(q_norm_b), gkb=tile_gain(k_norm_b),
        bias_a=_natten_bias(rpb_a).astype(_BF16), bias_b=_dilated_bias(t5_bias),
        wa=w_branch_a.astype(_BF16), wb=w_branch_b.astype(_BF16), wo=w_out.astype(_BF16),
        gmlp=norm_mlp.astype(_F32).reshape(1, D_MODEL),
        wup=w_up.astype(_BF16), wdn=w_down.astype(_BF16))


def kernel(x_prompt, x_sample, norm_mix, w_in, q_norm_a, k_norm_a, q_norm_b, k_norm_b, rpb_a,
           t5_bias, w_branch_a, w_branch_b, w_out, norm_mlp, w_up, w_down):
    y_prompt, y_sample = x_prompt, x_sample
    for l in range(norm_mix.shape[0]):
        p = _layer_params(norm_mix[l], w_in[l], q_norm_a[l], k_norm_a[l], q_norm_b[l],
                          k_norm_b[l], rpb_a[l], t5_bias, w_branch_a[l], w_branch_b[l],
                          w_out[l], norm_mlp[l], w_up[l], w_down[l])
        y_prompt = _encoder_layer(y_prompt, p)
        y_sample = _encoder_layer(y_sample, p)
    return (y_prompt, y_sample)
```

```python
import functools
import math

import numpy as np
import jax
import jax.numpy as jnp
from jax import lax
from jax.experimental import pallas as pl
from jax.experimental.pallas import tpu as pltpu

D_MODEL = 1024
HEAD_DIM = 64
H_A = 8
N_GROUPS_B = 3
HG_B = 4
H_B = N_GROUPS_B * HG_B
WINDOWS_B = (128, 512, 2048)
DILATIONS_B = (1, 4, 16)
GRID_W = 64
WIN_ROWS = 8
WIN_COLS = 16
NUM_BUCKETS = 32
T5_MAX_DIST = 1024
D_FF = 4 * D_MODEL
EPS = 1e-6
NEG = -1e30
LOG2E = math.log2(math.e)
QA_W = H_A * HEAD_DIM
QB_W = H_B * HEAD_DIM
GB_W = HG_B * HEAD_DIM
IN_W = 3 * QA_W + 3 * QB_W + 2 * D_MODEL
HALF_KEYS = 64
assert all((w // 2) // d == HALF_KEYS for w, d in zip(WINDOWS_B, DILATIONS_B))

LANES = 128
NORM_CHUNK = 256
TM_PROJ = 1024
DEINT_STEP = 4
PROJ_BLOCK = 1024
TM_MLP = 1024
FF_CHUNK = 1024
QROWS_A = 4
KROWS_A = QROWS_A + 8
BLOCKS_PER_STEP_A = 32
QBLK_B = 128
KBLK_B = QBLK_B + 2 * HALF_KEYS
CHUNK_B = 2048
VMEM_BYTES_V7X = 64 * 1024 * 1024
VMEM_LIMIT = VMEM_BYTES_V7X - 8 * 1024 * 1024
VMEM_LIMIT_DENSE = VMEM_BYTES_V7X - 4 * 1024 * 1024

_F32 = jnp.float32
_BF16 = jnp.bfloat16


def _resident(shape):
    nd = len(shape)
    return pl.BlockSpec(shape, lambda *_: (0,) * nd, pipeline_mode=pl.Buffered(1))


def _proj_kernel(x_ref, gmix_ref, w_ref, gqa_ref, gka_ref, gqb_ref, gkb_ref,
                 qa_ref, ka_ref, va_ref, b0_ref, b1_ref, b2_ref, gate_ref, scr_ref, scr2_ref):
    tm = x_ref.shape[1]
    x = x_ref[0]
    ms = jnp.mean(x * x, axis=-1, keepdims=True)
    h = (x * lax.rsqrt(ms + EPS) * gmix_ref[...]).astype(_BF16)

    wide = {}

    def chunk(c0):
        b0 = (c0 // PROJ_BLOCK) * PROJ_BLOCK
        if b0 not in wide:
            b1 = min(b0 + PROJ_BLOCK, IN_W)
            wide[b0] = jnp.dot(h, w_ref[:, b0:b1], preferred_element_type=_F32)
        return wide[b0][:, c0 - b0:c0 - b0 + NORM_CHUNK]

    lane = lax.broadcasted_iota(jnp.int32, (tm, LANES), 1)
    first = lane < HEAD_DIM

    def head_norm(t, gain_ref, scale):
        parts = []
        for c in range(t.shape[1] // LANES):
            tc = t[:, c * LANES:(c + 1) * LANES]
            sq = tc * tc
            s0 = jnp.sum(jnp.where(first, sq, 0.0), axis=-1, keepdims=True)
            s1 = jnp.sum(jnp.where(first, 0.0, sq), axis=-1, keepdims=True)
            msq = jnp.where(first, s0, s1) * (1.0 / HEAD_DIM)
            parts.append(tc * lax.rsqrt(msq + EPS))
        return jnp.concatenate(parts, axis=1) * (gain_ref[...] * scale)

    qk_scale = HEAD_DIM ** -0.5 * LOG2E

    for c in range(QA_W // NORM_CHUNK):
        sl = slice(c * NORM_CHUNK, (c + 1) * NORM_CHUNK)
        qa_ref[0, :, sl] = head_norm(chunk(sl.start), gqa_ref, qk_scale).astype(_BF16)
        ka_ref[0, :, sl] = head_norm(chunk(QA_W + sl.start), gka_ref, 1.0).astype(_BF16)
        va_ref[0, :, sl] = chunk(2 * QA_W + sl.start).astype(_BF16)

    outs = (b0_ref, b1_ref, b2_ref)
    base = 3 * QA_W
    slab = slab2 = 0
    for kind in range(3):
        for g, dil in enumerate(DILATIONS_B):
            tg = chunk(base + kind * QB_W + g * GB_W)
            if kind == 0:
                tg = head_norm(tg, gqb_ref, qk_scale)
            elif kind == 1:
                tg = head_norm(tg, gkb_ref, 1.0)
            lane0 = kind * GB_W
            if dil == 1:
                outs[g][0, 0, :, lane0:lane0 + GB_W] = tg.astype(_BF16)
                continue
            for half in range(GB_W // LANES):
                l0 = lane0 + half * LANES
                scr_ref[slab] = tg[:, half * LANES:(half + 1) * LANES]
                if dil == DEINT_STEP:
                    for r in range(dil):
                        sub = scr_ref[slab, pl.ds(r, tm // dil, stride=dil), :]
                        outs[g][0, r, :, l0:l0 + LANES] = sub.astype(_BF16)
                else:
                    assert dil == DEINT_STEP * DEINT_STEP
                    for lo in range(DEINT_STEP):
                        scr2_ref[slab2, lo] = scr_ref[
                            slab, pl.ds(lo, tm // DEINT_STEP, stride=DEINT_STEP), :]
                    for lo in range(DEINT_STEP):
                        for hi in range(DEINT_STEP):
                            sub = scr2_ref[slab2, lo, pl.ds(hi, tm // dil, stride=DEINT_STEP), :]
                            outs[g][0, hi * DEINT_STEP + lo, :, l0:l0 + LANES] = sub.astype(_BF16)
                    slab2 += 1
                slab += 1

    base = 3 * QA_W + 3 * QB_W
    for c in range(2 * D_MODEL // NORM_CHUNK):
        sl = slice(c * NORM_CHUNK, (c + 1) * NORM_CHUNK)
        gate = 0.5 * jnp.tanh(0.5 * chunk(base + sl.start)) + 0.5
        gate_ref[0, :, sl] = gate.astype(_BF16)


def _proj_call(x, gmix, w_in, gqa, gka, gqb, gkb):
    B, T, D = x.shape
    tm = TM_PROJ
    assert T % tm == 0 and tm % (16 * max(DILATIONS_B)) == 0
    n_slabs = 3 * sum(1 for d in DILATIONS_B if d > 1) * (GB_W // LANES)
    n_slabs2 = 3 * sum(1 for d in DILATIONS_B if d > DEINT_STEP) * (GB_W // LANES)
    tok = lambda w: pl.BlockSpec((1, tm, w), lambda b, i: (b, i, 0))
    sub = lambda d: pl.BlockSpec((1, d, tm // d, 3 * GB_W), lambda b, i: (b, 0, i, 0))
    out_shape = (
        jax.ShapeDtypeStruct((B, T, QA_W), _BF16),
        jax.ShapeDtypeStruct((B, T, QA_W), _BF16),
        jax.ShapeDtypeStruct((B, T, QA_W), _BF16),
    ) + tuple(jax.ShapeDtypeStruct((B, d, T // d, 3 * GB_W), _BF16) for d in DILATIONS_B) + (
        jax.ShapeDtypeStruct((B, T, 2 * D_MODEL), _BF16),
    )
    return pl.pallas_call(
        _proj_kernel,
        grid=(B, T // tm),
        in_specs=[tok(D), _resident((1, D)), _resident((D, IN_W)),
                  _resident((1, NORM_CHUNK)), _resident((1, NORM_CHUNK)),
                  _resident((1, NORM_CHUNK)), _resident((1, NORM_CHUNK))],
        out_specs=(tok(QA_W), tok(QA_W), tok(QA_W)) + tuple(sub(d) for d in DILATIONS_B)
        + (tok(2 * D_MODEL),),
        out_shape=out_shape,
        scratch_shapes=[pltpu.VMEM((n_slabs, tm, LANES), _F32),
                        pltpu.VMEM((n_slabs2, DEINT_STEP, tm // DEINT_STEP, LANES), _F32)],
        compiler_params=pltpu.CompilerParams(
            dimension_semantics=("arbitrary", "arbitrary"), vmem_limit_bytes=VMEM_LIMIT_DENSE),
        name="proj",
    )(x, gmix, w_in, gqa, gka, gqb, gkb)


def _pair_attention(q, kw, vw, bias2):
    m_rows = q.shape[0]
    lane = lax.broadcasted_iota(jnp.int32, (m_rows, LANES), 1)
    first = lane < HEAD_DIM
    zero = jnp.zeros_like(q)
    q2 = jnp.concatenate([jnp.where(first, q, zero), jnp.where(first, zero, q)], axis=0)
    s = lax.dot_general(q2, kw, (((1,), (1,)), ((), ())), preferred_element_type=_F32)
    s = s + bias2.astype(_F32)
    m = jnp.max(s, axis=-1, keepdims=True)
    p = jnp.exp2(s - m)
    o2l = jnp.dot(p.astype(_BF16), jnp.concatenate([vw, jnp.ones_like(vw)], axis=1),
                  preferred_element_type=_F32)
    o2, l = o2l[:, :LANES], o2l[:, LANES:]

    def pick(x2):
        x2 = jnp.broadcast_to(x2, (2 * m_rows, LANES))
        return jnp.where(first, x2[:m_rows], x2[m_rows:])

    return pick(o2), pick(m), pick(l)


def _toeplitz(v, n_rows, n_cols, center):
    n = v.shape[-1]
    period = n_rows + n_cols - 1
    left = n_rows - 1 - center
    cfg = [(0, 0, 0)] * (v.ndim - 1) + [(left, period - n - left, 0)]
    ext = lax.pad(v, jnp.asarray(NEG, v.dtype), cfg)
    w = jnp.concatenate([ext[..., n_rows - 1:], ext[..., :n_rows - 1]], axis=-1)
    flat = jnp.tile(w, (1,) * (v.ndim - 1) + (n_rows,))[..., :n_rows * (period - 1)]
    return flat.reshape(v.shape[:-1] + (n_rows, period - 1))[..., :n_cols]


def _natten_kernel(q_ref, k_ref, v_ref, bias_ref, o_ref, *, rows):
    i = pl.program_id(2)
    mq, nk = QROWS_A * GRID_W, KROWS_A * GRID_W
    last_blk = rows // QROWS_A - 1
    blocks_per_step = q_ref.shape[1] // mq
    for u in range(blocks_per_step):
        blk = blocks_per_step * i + u
        krow0 = jnp.clip(QROWS_A * blk - WIN_ROWS // 2, 0, rows - KROWS_A)
        k0 = pl.multiple_of(krow0 * GRID_W, GRID_W)
        variant = jnp.where(blk == 0, 0, jnp.where(blk == last_blk, 2, 1))
        kw = k_ref[0, pl.ds(k0, nk), :]
        vw = v_ref[0, pl.ds(k0, nk), :]
        q = q_ref[0, u * mq:(u + 1) * mq, :]
        o, _, l = _pair_attention(q, kw, vw, bias_ref[variant, 0])
        o_ref[0, u * mq:(u + 1) * mq, :] = (o / l).astype(_BF16)


def _natten_bias(rpb):
    n_off = 2 * WIN_ROWS - 1
    toep = _toeplitz(rpb.astype(_F32) * LOG2E, GRID_W, GRID_W, WIN_COLS - 1)
    c = np.arange(GRID_W)[:, None]
    kc = np.arange(GRID_W)[None, :]
    col_start = np.clip(c - WIN_COLS // 2, 0, GRID_W - WIN_COLS)
    col_ok = (kc >= col_start) & (kc < col_start + WIN_COLS)
    col_bias = jnp.where(jnp.asarray(col_ok), toep, NEG)
    neg_block = jnp.full((H_A, GRID_W, GRID_W), NEG, _F32)
    fake_rows = 3 * KROWS_A
    nblk = fake_rows // QROWS_A
    variants = []
    for blk in (0, nblk // 2, nblk - 1):
        krow0 = int(np.clip(QROWS_A * blk - WIN_ROWS // 2, 0, fake_rows - KROWS_A))
        q_rows = []
        for rq in range(QROWS_A):
            r = QROWS_A * blk + rq
            start = int(np.clip(r - WIN_ROWS // 2, 0, fake_rows - WIN_ROWS))
            blocks = []
            for rk in range(KROWS_A):
                kr = krow0 + rk
                row_off = kr - r + WIN_ROWS - 1
                inside = start <= kr < start + WIN_ROWS
                assert not inside or 0 <= row_off < n_off
                blocks.append(col_bias[:, row_off] if inside else neg_block)
            q_rows.append(jnp.concatenate(blocks, axis=-1))
        variants.append(jnp.concatenate(q_rows, axis=-2))
    tiles = jnp.stack(variants)
    return tiles.reshape(3, H_A // 2, 2 * QROWS_A * GRID_W, KROWS_A * GRID_W)


def _natten_call(qa, ka, va, bias):
    B, T, _ = qa.shape
    rows = T // GRID_W
    rows_per_step = QROWS_A * min(BLOCKS_PER_STEP_A, rows // QROWS_A)
    assert T % GRID_W == 0 and rows % rows_per_step == 0 and rows >= KROWS_A
    mq, nk = rows_per_step * GRID_W, KROWS_A * GRID_W
    return pl.pallas_call(
        functools.partial(_natten_kernel, rows=rows),
        grid=(B, H_A // 2, rows // rows_per_step),
        in_specs=[pl.BlockSpec((1, mq, LANES), lambda b, p, i: (b, i, p)),
                  pl.BlockSpec((1, T, LANES), lambda b, p, i: (b, 0, p)),
                  pl.BlockSpec((1, T, LANES), lambda b, p, i: (b, 0, p)),
                  pl.BlockSpec((3, 1, 2 * QROWS_A * GRID_W, nk), lambda b, p, i: (0, p, 0, 0))],
        out_specs=pl.BlockSpec((1, mq, LANES), lambda b, p, i: (b, i, p)),
        out_shape=jax.ShapeDtypeStruct((B, T, QA_W), _BF16),
        compiler_params=pltpu.CompilerParams(
            dimension_semantics=("arbitrary", "arbitrary", "arbitrary"),
            vmem_limit_bytes=VMEM_LIMIT),
        name="natten",
    )(qa, ka, va, bias)


def _dilated_kernel(q0_ref, k0_ref, v0_ref, q1_ref, k1_ref, v1_ref, q2_ref, k2_ref, v2_ref,
                    bias_ref, o_ref, so_ref, sm_ref, sl_ref, stage_ref):
    j = pl.program_id(2)
    chunk = o_ref.shape[1]
    refs = ((q0_ref, k0_ref, v0_ref), (q1_ref, k1_ref, v1_ref), (q2_ref, k2_ref, v2_ref))

    def tile(g, r, sb):
        dil = DILATIONS_B[g]
        q_ref, k_ref, v_ref = refs[g]
        sub_len = k_ref.shape[2]
        lq0 = j * (chunk // dil) + sb * QBLK_B
        kl0 = jnp.clip(lq0 - HALF_KEYS, 0, sub_len - KBLK_B)
        variant = (lq0 - kl0) // HALF_KEYS
        kl0 = pl.multiple_of(kl0, HALF_KEYS)
        q = q_ref[0, r, sb * QBLK_B:(sb + 1) * QBLK_B, :]
        kw = k_ref[0, r, pl.ds(kl0, KBLK_B), :]
        vw = v_ref[0, r, pl.ds(kl0, KBLK_B), :]
        res = _pair_attention(q, kw, vw, bias_ref[0, g, variant])
        if dil > DEINT_STEP:
            lo, hi = r % DEINT_STEP, r // DEINT_STEP
            rows_out = pl.ds(sb * QBLK_B * DEINT_STEP + hi, QBLK_B, stride=DEINT_STEP)
            for a, val in enumerate(res):
                stage_ref[a, lo, rows_out, :] = val
            return
        if dil == 1:
            rows_out = pl.ds(sb * QBLK_B, QBLK_B)
        else:
            rows_out = pl.ds(sb * QBLK_B * dil + r, QBLK_B, stride=dil)
        for dst, val in zip((so_ref, sm_ref, sl_ref), res):
            dst[g, rows_out, :] = val

    def flush_stage(g, lo):
        rows_out = pl.ds(lo, chunk // DEINT_STEP, stride=DEINT_STEP)
        for a, dst in enumerate((so_ref, sm_ref, sl_ref)):
            dst[g, rows_out, :] = stage_ref[a, lo]

    def combine(rows):
        m_all = jnp.maximum(jnp.maximum(sm_ref[0, rows, :], sm_ref[1, rows, :]), sm_ref[2, rows, :])
        num = den = None
        for g in range(N_GROUPS_B):
            w = jnp.exp2(sm_ref[g, rows, :] - m_all)
            num = w * so_ref[g, rows, :] if num is None else num + w * so_ref[g, rows, :]
            den = w * sl_ref[g, rows, :] if den is None else den + w * sl_ref[g, rows, :]
        o_ref[0, rows, :] = (num / den).astype(_BF16)

    for g in reversed(range(N_GROUPS_B)):
        dil = DILATIONS_B[g]
        nsb = chunk // dil // QBLK_B
        if dil > DEINT_STEP:
            assert dil == DEINT_STEP * DEINT_STEP
            for lo in range(DEINT_STEP):
                for hi in range(DEINT_STEP):
                    for sb in range(nsb):
                        tile(g, hi * DEINT_STEP + lo, sb)
                flush_stage(g, lo)
            continue
        for r in range(dil):
            for sb in range(nsb):
                tile(g, r, sb)
                if dil == 1:
                    combine(slice(sb * QBLK_B, (sb + 1) * QBLK_B))


def _t5_buckets(rel):
    half = NUM_BUCKETS // 2
    ret = np.where(rel > 0, half, 0)
    n = np.abs(rel)
    max_exact = half // 2
    large = max_exact + (np.log(np.maximum(n, 1) / max_exact)
                         / np.log(T5_MAX_DIST / max_exact) * (half - max_exact)).astype(np.int32)
    large = np.minimum(large, half - 1)
    return (ret + np.where(n < max_exact, n, large)).astype(np.int32)


def _dilated_bias(t5_bias):
    rel = np.arange(-HALF_KEYS, HALF_KEYS + 1)
    tab = t5_bias.astype(_F32).T * LOG2E
    by_rel = jnp.stack([
        jnp.take(tab[g * HG_B:(g + 1) * HG_B], jnp.asarray(_t5_buckets(rel * dil)), axis=1)
        for g, dil in enumerate(DILATIONS_B)])
    tiles = jnp.stack([
        _toeplitz(by_rel, QBLK_B, KBLK_B, HALF_KEYS - variant * HALF_KEYS)
        for variant in range(3)])
    tiles = tiles.reshape(3, N_GROUPS_B, HG_B // 2, 2 * QBLK_B, KBLK_B)
    return jnp.transpose(tiles, (2, 1, 0, 3, 4))


def _dilated_call(b0, b1, b2, bias):
    B = b0.shape[0]
    T = b0.shape[2]
    chunk = min(CHUNK_B, T)
    assert T % chunk == 0 and chunk % (QBLK_B * max(DILATIONS_B)) == 0
    assert T // max(DILATIONS_B) >= KBLK_B and DILATIONS_B[0] == 1
    in_specs, args = [], []
    for arr, dil in zip((b0, b1, b2), DILATIONS_B):
        sub_len = T // dil
        in_specs += [
            pl.BlockSpec((1, dil, chunk // dil, LANES), lambda b, p, j: (b, 0, j, p)),
            pl.BlockSpec((1, dil, sub_len, LANES), lambda b, p, j: (b, 0, 0, 2 + p)),
            pl.BlockSpec((1, dil, sub_len, LANES), lambda b, p, j: (b, 0, 0, 4 + p)),
        ]
        args += [arr, arr, arr]
    in_specs.append(pl.BlockSpec((1, N_GROUPS_B, 3, 2 * QBLK_B, KBLK_B),
                                 lambda b, p, j: (p, 0, 0, 0, 0)))
    args.append(bias)
    return pl.pallas_call(
        _dilated_kernel,
        grid=(B, HG_B // 2, T // chunk),
        in_specs=in_specs,
        out_specs=pl.BlockSpec((1, chunk, LANES), lambda b, p, j: (b, j, p)),
        out_shape=jax.ShapeDtypeStruct((B, T, GB_W), _BF16),
        scratch_shapes=[pltpu.VMEM((N_GROUPS_B, chunk, LANES), _F32)] * 3
        + [pltpu.VMEM((3, DEINT_STEP, chunk // DEINT_STEP, LANES), _F32)],
        compiler_params=pltpu.CompilerParams(
            dimension_semantics=("arbitrary", "arbitrary", "arbitrary"),
            vmem_limit_bytes=VMEM_LIMIT),
        name="dilated",
    )(*args)


def _mlp_kernel(x_ref, oa_ref, ob_ref, gate_ref, wa_ref, wb_ref, wo_ref, gmlp_ref, wup_ref,
                wdn_ref, y_ref):
    ya = jnp.dot(oa_ref[...], wa_ref[...], preferred_element_type=_F32)
    yb = jnp.dot(ob_ref[...], wb_ref[...], preferred_element_type=_F32)
    ga = gate_ref[:, :D_MODEL].astype(_F32)
    gb = gate_ref[:, D_MODEL:].astype(_F32)
    merged = (ga * ya + gb * yb).astype(_BF16)
    x1 = x_ref[...] + jnp.dot(merged, wo_ref[...], preferred_element_type=_F32)
    ms = jnp.mean(x1 * x1, axis=-1, keepdims=True)
    hm = (x1 * lax.rsqrt(ms + EPS) * gmlp_ref[...]).astype(_BF16)
    acc = x1
    for c in range(D_FF // FF_CHUNK):
        sl = slice(c * FF_CHUNK, (c + 1) * FF_CHUNK)
        u = jnp.maximum(jnp.dot(hm, wup_ref[:, sl], preferred_element_type=_F32), 0.0)
        acc = acc + jnp.dot((u * u).astype(_BF16), wdn_ref[sl, :], preferred_element_type=_F32)
    y_ref[...] = acc


def _mlp_call(x2, oa2, ob2, gates2, wa, wb, wo, gmlp, wup, wdn):
    n_tok, D = x2.shape
    tm = TM_MLP
    assert n_tok % tm == 0
    tok = lambda w: pl.BlockSpec((tm, w), lambda i: (i, 0))
    return pl.pallas_call(
        _mlp_kernel,
        grid=(n_tok // tm,),
        in_specs=[tok(D), tok(QA_W), tok(GB_W), tok(2 * D_MODEL),
                  _resident((QA_W, D)), _resident((GB_W, D)), _resident((D, D)),
                  _resident((1, D)), _resident((D, D_FF)), _resident((D_FF, D))],
        out_specs=tok(D),
        out_shape=jax.ShapeDtypeStruct((n_tok, D), _F32),
        compiler_params=pltpu.CompilerParams(
            dimension_semantics=("arbitrary",), vmem_limit_bytes=VMEM_LIMIT_DENSE),
        name="mlp",
    )(x2, oa2, ob2, gates2, wa, wb, wo, gmlp, wup, wdn)


def _encoder_layer(x, p):
    B, T, D = x.shape
    qa, ka, va, b0, b1, b2, gates = _proj_call(
        x, p["gmix"], p["w_in"], p["gqa"], p["gka"], p["gqb"], p["gkb"])
    oa = _natten_call(qa, ka, va, p["bias_a"])
    ob = _dilated_call(b0, b1, b2, p["bias_b"])
    y = _mlp_call(x.reshape(B * T, D), oa.reshape(B * T, QA_W), ob.reshape(B * T, GB_W),
                  gates.reshape(B * T, 2 * D_MODEL), p["wa"], p["wb"], p["wo"], p["gmlp"],
                  p["wup"], p["wdn"])
    return y.reshape(B, T, D)


def _layer_params(norm_mix, w_in, q_norm_a, k_norm_a, q_norm_b, k_norm_b, rpb_a, t5_bias,
                  w_branch_a, w_branch_b, w_out, norm_mlp, w_up, w_down):
    tile_gain = lambda g: jnp.tile(g.astype(_F32), NORM_CHUNK // HEAD_DIM).reshape(1, NORM_CHUNK)
    return dict(
        gmix=norm_mix.astype(_F32).reshape(1, D_MODEL), w_in=w_in.astype(_BF16),
        gqa=tile_gain(q_norm_a), gka=tile_gain(k_norm_a),
        gqb=tile_gain(q_norm_b), gkb=tile_gain(k_norm_b),
        bias_a=_natten_bias(rpb_a).astype(_BF16), bias_b=_dilated_bias(t5_bias),
        wa=w_branch_a.astype(_BF16), wb=w_branch_b.astype(_BF16), wo=w_out.astype(_BF16),
        gmlp=norm_mlp.astype(_F32).reshape(1, D_MODEL),
        wup=w_up.astype(_BF16), wdn=w_down.astype(_BF16))


def kernel(x_prompt, x_sample, norm_mix, w_in, q_norm_a, k_norm_a, q_norm_b, k_norm_b, rpb_a,
           t5_bias, w_branch_a, w_branch_b, w_out, norm_mlp, w_up, w_down):
    y_prompt, y_sample = x_prompt, x_sample
    for l in range(norm_mix.shape[0]):
        p = _layer_params(norm_mix[l], w_in[l], q_norm_a[l], k_norm_a[l], q_norm_b[l],
                          k_norm_b[l], rpb_a[l], t5_bias, w_branch_a[l], w_branch_b[l],
                          w_out[l], norm_mlp[l], w_up[l], w_down[l])
        y_prompt = _encoder_layer(y_prompt, p)
        y_sample = _encoder_layer(y_sample, p)
    return (y_prompt, y_sample)
```

```python
import functools
import math

import numpy as np
import jax
import jax.numpy as jnp
from jax import lax
from jax.experimental import pallas as pl
from jax.experimental.pallas import tpu as pltpu

D_MODEL = 1024
HEAD_DIM = 64
H_A = 8
N_GROUPS_B = 3
HG_B = 4
H_B = N_GROUPS_B * HG_B
WINDOWS_B = (128, 512, 2048)
DILATIONS_B = (1, 4, 16)
GRID_W = 64
WIN_ROWS = 8
WIN_COLS = 16
NUM_BUCKETS = 32
T5_MAX_DIST = 1024
D_FF = 4 * D_MODEL
EPS = 1e-6
NEG = -1e30
LOG2E = math.log2(math.e)
QA_W = H_A * HEAD_DIM
QB_W = H_B * HEAD_DIM
GB_W = HG_B * HEAD_DIM
IN_W = 3 * QA_W + 3 * QB_W + 2 * D_MODEL
HALF_KEYS = 64
assert all((w // 2) // d == HALF_KEYS for w, d in zip(WINDOWS_B, DILATIONS_B))

LANES = 128
NORM_CHUNK = 256
TM_PROJ = 1024
DEINT_STEP = 4
PROJ_BLOCK = 1024
TM_MLP = 1024
FF_CHUNK = 1024
QROWS_A = 4
KROWS_A = QROWS_A + 8
BLOCKS_PER_STEP_A = 32
QBLK_B = 128
KBLK_B = QBLK_B + 2 * HALF_KEYS
CHUNK_B = 2048
VMEM_BYTES_V7X = 64 * 1024 * 1024
VMEM_LIMIT = VMEM_BYTES_V7X - 8 * 1024 * 1024
VMEM_LIMIT_DENSE = VMEM_BYTES_V7X - 4 * 1024 * 1024

_F32 = jnp.float32
_BF16 = jnp.bfloat16


def _resident(shape):
    nd = len(shape)
    return pl.BlockSpec(shape, lambda *_: (0,) * nd, pipeline_mode=pl.Buffered(1))


def _proj_kernel(x_ref, gmix_ref, w_ref, gqa_ref, gka_ref, gqb_ref, gkb_ref,
                 qa_ref, ka_ref, va_ref, b0_ref, b1_ref, b2_ref, gate_ref, scr_ref, scr2_ref):
    tm = x_ref.shape[1]
    x = x_ref[0]
    ms = jnp.mean(x * x, axis=-1, keepdims=True)
    h = (x * lax.rsqrt(ms + EPS) * gmix_ref[...]).astype(_BF16)

    wide = {}

    def chunk(c0):
        b0 = (c0 // PROJ_BLOCK) * PROJ_BLOCK
        if b0 not in wide:
            b1 = min(b0 + PROJ_BLOCK, IN_W)
            wide[b0] = jnp.dot(h, w_ref[:, b0:b1], preferred_element_type=_F32)
        return wide[b0][:, c0 - b0:c0 - b0 + NORM_CHUNK]

    lane = lax.broadcasted_iota(jnp.int32, (tm, LANES), 1)
    first = lane < HEAD_DIM

    def head_norm(t, gain_ref, scale):
        parts = []
        for c in range(t.shape[1] // LANES):
            tc = t[:, c * LANES:(c + 1) * LANES]
            sq = tc * tc
            s0 = jnp.sum(jnp.where(first, sq, 0.0), axis=-1, keepdims=True)
            s1 = jnp.sum(jnp.where(first, 0.0, sq), axis=-1, keepdims=True)
            msq = jnp.where(first, s0, s1) * (1.0 / HEAD_DIM)
            parts.append(tc * lax.rsqrt(msq + EPS))
        return jnp.concatenate(parts, axis=1) * (gain_ref[...] * scale)

    qk_scale = HEAD_DIM ** -0.5 * LOG2E

    for c in range(QA_W // NORM_CHUNK):
        sl = slice(c * NORM_CHUNK, (c + 1) * NORM_CHUNK)
        qa_ref[0, :, sl] = head_norm(chunk(sl.start), gqa_ref, qk_scale).astype(_BF16)
        ka_ref[0, :, sl] = head_norm(chunk(QA_W + sl.start), gka_ref, 1.0).astype(_BF16)
        va_ref[0, :, sl] = chunk(2 * QA_W + sl.start).astype(_BF16)

    outs = (b0_ref, b1_ref, b2_ref)
    base = 3 * QA_W
    slab = slab2 = 0
    for kind in range(3):
        for g, dil in enumerate(DILATIONS_B):
            tg = chunk(base + kind * QB_W + g * GB_W)
            if kind == 0:
                tg = head_norm(tg, gqb_ref, qk_scale)
            elif kind == 1:
                tg = head_norm(tg, gkb_ref, 1.0)
            lane0 = kind * GB_W
            if dil == 1:
                outs[g][0, 0, :, lane0:lane0 + GB_W] = tg.astype(_BF16)
                continue
            for half in range(GB_W // LANES):
                l0 = lane0 + half * LANES
                scr_ref[slab] = tg[:, half * LANES:(half + 1) * LANES]
                if dil == DEINT_STEP:
                    for r in range(dil):
                        sub = scr_ref[slab, pl.ds(r, tm // dil, stride=dil), :]
                        outs[g][0, r, :, l0:l0 + LANES] = sub.astype(_BF16)
                else:
                    assert dil == DEINT_STEP * DEINT_STEP
                    for lo in range(DEINT_STEP):
                        scr2_ref[slab2, lo] = scr_ref[
                            slab, pl.ds(lo, tm // DEINT_STEP, stride=DEINT_STEP), :]
                    for lo in range(DEINT_STEP):
                        for hi in range(DEINT_STEP):
                            sub = scr2_ref[slab2, lo, pl.ds(hi, tm // dil, stride=DEINT_STEP), :]
                            outs[g][0, hi * DEINT_STEP + lo, :, l0:l0 + LANES] = sub.astype(_BF16)
                    slab2 += 1
                slab += 1

    base = 3 * QA_W + 3 * QB_W
    for c in range(2 * D_MODEL // NORM_CHUNK):
        sl = slice(c * NORM_CHUNK, (c + 1) * NORM_CHUNK)
        gate = 0.5 * jnp.tanh(0.5 * chunk(base + sl.start)) + 0.5
        gate_ref[0, :, sl] = gate.astype(_BF16)


def _proj_call(x, gmix, w_in, gqa, gka, gqb, gkb):
    B, T, D = x.shape
    tm = TM_PROJ
    assert T % tm == 0 and tm % (16 * max(DILATIONS_B)) == 0
    n_slabs = 3 * sum(1 for d in DILATIONS_B if d > 1) * (GB_W // LANES)
    n_slabs2 = 3 * sum(1 for d in DILATIONS_B if d > DEINT_STEP) * (GB_W // LANES)
    tok = lambda w: pl.BlockSpec((1, tm, w), lambda b, i: (b, i, 0))
    sub = lambda d: pl.BlockSpec((1, d, tm // d, 3 * GB_W), lambda b, i: (b, 0, i, 0))
    out_shape = (
        jax.ShapeDtypeStruct((B, T, QA_W), _BF16),
        jax.ShapeDtypeStruct((B, T, QA_W), _BF16),
        jax.ShapeDtypeStruct((B, T, QA_W), _BF16),
    ) + tuple(jax.ShapeDtypeStruct((B, d, T // d, 3 * GB_W), _BF16) for d in DILATIONS_B) + (
        jax.ShapeDtypeStruct((B, T, 2 * D_MODEL), _BF16),
    )
    return pl.pallas_call(
        _proj_kernel,
        grid=(B, T // tm),
        in_specs=[tok(D), _resident((1, D)), _resident((D, IN_W)),
                  _resident((1, NORM_CHUNK)), _resident((1, NORM_CHUNK)),
                  _resident((1, NORM_CHUNK)), _resident((1, NORM_CHUNK))],
        out_specs=(tok(QA_W), tok(QA_W), tok(QA_W)) + tuple(sub(d) for d in DILATIONS_B)
        + (tok(2 * D_MODEL),),
        out_shape=out_shape,
        scratch_shapes=[pltpu.VMEM((n_slabs, tm, LANES), _F32),
                        pltpu.VMEM((n_slabs2, DEINT_STEP, tm // DEINT_STEP, LANES), _F32)],
        compiler_params=pltpu.CompilerParams(
            dimension_semantics=("arbitrary", "arbitrary"), vmem_limit_bytes=VMEM_LIMIT_DENSE),
        name="proj",
    )(x, gmix, w_in, gqa, gka, gqb, gkb)


def _pair_attention(q, kw, vw, bias2):
    m_rows = q.shape[0]
    lane = lax.broadcasted_iota(jnp.int32, (m_rows, LANES), 1)
    first = lane < HEAD_DIM
    zero = jnp.zeros_like(q)
    q2 = jnp.concatenate([jnp.where(first, q, zero), jnp.where(first, zero, q)], axis=0)
    s = lax.dot_general(q2, kw, (((1,), (1,)), ((), ())), preferred_element_type=_F32)
    s = s + bias2.astype(_F32)
    m = jnp.max(s, axis=-1, keepdims=True)
    p = jnp.exp2(s - m)
    o2l = jnp.dot(p.astype(_BF16), jnp.concatenate([vw, jnp.ones_like(vw)], axis=1),
                  preferred_element_type=_F32)
    o2, l = o2l[:, :LANES], o2l[:, LANES:]

    def pick(x2):
        x2 = jnp.broadcast_to(x2, (2 * m_rows, LANES))
        return jnp.where(first, x2[:m_rows], x2[m_rows:])

    return pick(o2), pick(m), pick(l)


def _toeplitz(v, n_rows, n_cols, center):
    n = v.shape[-1]
    period = n_rows + n_cols - 1
    left = n_rows - 1 - center
    cfg = [(0, 0, 0)] * (v.ndim - 1) + [(left, period - n - left, 0)]
    ext = lax.pad(v, jnp.asarray(NEG, v.dtype), cfg)
    w = jnp.concatenate([ext[..., n_rows - 1:], ext[..., :n_rows - 1]], axis=-1)
    flat = jnp.tile(w, (1,) * (v.ndim - 1) + (n_rows,))[..., :n_rows * (period - 1)]
    return flat.reshape(v.shape[:-1] + (n_rows, period - 1))[..., :n_cols]


def _natten_kernel(q_ref, k_ref, v_ref, bias_ref, o_ref, *, rows):
    i = pl.program_id(2)
    mq, nk = QROWS_A * GRID_W, KROWS_A * GRID_W
    last_blk = rows // QROWS_A - 1
    blocks_per_step = q_ref.shape[1] // mq
    for u in range(blocks_per_step):
        blk = blocks_per_step * i + u
        krow0 = jnp.clip(QROWS_A * blk - WIN_ROWS // 2, 0, rows - KROWS_A)
        k0 = pl.multiple_of(krow0 * GRID_W, GRID_W)
        variant = jnp.where(blk == 0, 0, jnp.where(blk == last_blk, 2, 1))
        kw = k_ref[0, pl.ds(k0, nk), :]
        vw = v_ref[0, pl.ds(k0, nk), :]
        q = q_ref[0, u * mq:(u + 1) * mq, :]
        o, _, l = _pair_attention(q, kw, vw, bias_ref[variant, 0])
        o_ref[0, u * mq:(u + 1) * mq, :] = (o / l).astype(_BF16)


def _natten_bias(rpb):
    n_off = 2 * WIN_ROWS - 1
    c = np.arange(GRID_W)[:, None]
    kc = np.arange(GRID_W)[None, :]
    pick = (kc - c + WIN_COLS - 1)[None] == np.arange(2 * WIN_COLS - 1)[:, None, None]
    toep = jnp.einsum("hdo,ock->hdck", rpb.astype(_F32) * LOG2E,
                      jnp.asarray(pick.astype(np.float32)), precision=lax.Precision.HIGHEST)
    col_start = np.clip(c - WIN_COLS // 2, 0, GRID_W - WIN_COLS)
    col_ok = (kc >= col_start) & (kc < col_start + WIN_COLS)
    col_bias = jnp.where(jnp.asarray(col_ok), toep, NEG)
    neg_block = jnp.full((H_A, GRID_W, GRID_W), NEG, _F32)
    fake_rows = 3 * KROWS_A
    nblk = fake_rows // QROWS_A
    variants = []
    for blk in (0, nblk // 2, nblk - 1):
        krow0 = int(np.clip(QROWS_A * blk - WIN_ROWS // 2, 0, fake_rows - KROWS_A))
        q_rows = []
        for rq in range(QROWS_A):
            r = QROWS_A * blk + rq
            start = int(np.clip(r - WIN_ROWS // 2, 0, fake_rows - WIN_ROWS))
            blocks = []
            for rk in range(KROWS_A):
                kr = krow0 + rk
                row_off = kr - r + WIN_ROWS - 1
                inside = start <= kr < start + WIN_ROWS
                assert not inside or 0 <= row_off < n_off
                blocks.append(col_bias[:, row_off] if inside else neg_block)
            q_rows.append(jnp.concatenate(blocks, axis=-1))
        variants.append(jnp.concatenate(q_rows, axis=-2))
    tiles = jnp.stack(variants)
    return tiles.reshape(3, H_A // 2, 2 * QROWS_A * GRID_W, KROWS_A * GRID_W)


def _natten_call(qa, ka, va, bias):
    B, T, _ = qa.shape
    rows = T // GRID_W
    rows_per_step = QROWS_A * min(BLOCKS_PER_STEP_A, rows // QROWS_A)
    assert T % GRID_W == 0 and rows % rows_per_step == 0 and rows >= KROWS_A
    mq, nk = rows_per_step * GRID_W, KROWS_A * GRID_W
    return pl.pallas_call(
        functools.partial(_natten_kernel, rows=rows),
        grid=(B, H_A // 2, rows // rows_per_step),
        in_specs=[pl.BlockSpec((1, mq, LANES), lambda b, p, i: (b, i, p)),
                  pl.BlockSpec((1, T, LANES), lambda b, p, i: (b, 0, p)),
                  pl.BlockSpec((1, T, LANES), lambda b, p, i: (b, 0, p)),
                  pl.BlockSpec((3, 1, 2 * QROWS_A * GRID_W, nk), lambda b, p, i: (0, p, 0, 0))],
        out_specs=pl.BlockSpec((1, mq, LANES), lambda b, p, i: (b, i, p)),
        out_shape=jax.ShapeDtypeStruct((B, T, QA_W), _BF16),
        compiler_params=pltpu.CompilerParams(
            dimension_semantics=("arbitrary", "arbitrary", "arbitrary"),
            vmem_limit_bytes=VMEM_LIMIT),
        name="natten",
    )(qa, ka, va, bias)


def _dilated_kernel(q0_ref, k0_ref, v0_ref, q1_ref, k1_ref, v1_ref, q2_ref, k2_ref, v2_ref,
                    bias_ref, o_ref, so_ref, sm_ref, sl_ref, stage_ref):
    j = pl.program_id(2)
    chunk = o_ref.shape[1]
    refs = ((q0_ref, k0_ref, v0_ref), (q1_ref, k1_ref, v1_ref), (q2_ref, k2_ref, v2_ref))

    def tile(g, r, sb):
        dil = DILATIONS_B[g]
        q_ref, k_ref, v_ref = refs[g]
        sub_len = k_ref.shape[2]
        lq0 = j * (chunk // dil) + sb * QBLK_B
        kl0 = jnp.clip(lq0 - HALF_KEYS, 0, sub_len - KBLK_B)
        variant = (lq0 - kl0) // HALF_KEYS
        kl0 = pl.multiple_of(kl0, HALF_KEYS)
        q = q_ref[0, r, sb * QBLK_B:(sb + 1) * QBLK_B, :]
        kw = k_ref[0, r, pl.ds(kl0, KBLK_B), :]
        vw = v_ref[0, r, pl.ds(kl0, KBLK_B), :]
        res = _pair_attention(q, kw, vw, bias_ref[0, g, variant])
        if dil > DEINT_STEP:
            lo, hi = r % DEINT_STEP, r // DEINT_STEP
            rows_out = pl.ds(sb * QBLK_B * DEINT_STEP + hi, QBLK_B, stride=DEINT_STEP)
            for a, val in enumerate(res):
                stage_ref[a, lo, rows_out, :] = val
            return
        if dil == 1:
            rows_out = pl.ds(sb * QBLK_B, QBLK_B)
        else:
            rows_out = pl.ds(sb * QBLK_B * dil + r, QBLK_B, stride=dil)
        for dst, val in zip((so_ref, sm_ref, sl_ref), res):
            dst[g, rows_out, :] = val

    def flush_stage(g, lo):
        rows_out = pl.ds(lo, chunk // DEINT_STEP, stride=DEINT_STEP)
        for a, dst in enumerate((so_ref, sm_ref, sl_ref)):
            dst[g, rows_out, :] = stage_ref[a, lo]

    def combine(rows):
        m_all = jnp.maximum(jnp.maximum(sm_ref[0, rows, :], sm_ref[1, rows, :]), sm_ref[2, rows, :])
        num = den = None
        for g in range(N_GROUPS_B):
            w = jnp.exp2(sm_ref[g, rows, :] - m_all)
            num = w * so_ref[g, rows, :] if num is None else num + w * so_ref[g, rows, :]
            den = w * sl_ref[g, rows, :] if den is None else den + w * sl_ref[g, rows, :]
        o_ref[0, rows, :] = (num / den).astype(_BF16)

    for g in reversed(range(N_GROUPS_B)):
        dil = DILATIONS_B[g]
        nsb = chunk // dil // QBLK_B
        if dil > DEINT_STEP:
            assert dil == DEINT_STEP * DEINT_STEP
            for lo in range(DEINT_STEP):
                for hi in range(DEINT_STEP):
                    for sb in range(nsb):
                        tile(g, hi * DEINT_STEP + lo, sb)
                flush_stage(g, lo)
            continue
        for r in range(dil):
            for sb in range(nsb):
                tile(g, r, sb)
                if dil == 1:
                    combine(slice(sb * QBLK_B, (sb + 1) * QBLK_B))


def _t5_buckets(rel):
    half = NUM_BUCKETS // 2
    ret = np.where(rel > 0, half, 0)
    n = np.abs(rel)
    max_exact = half // 2
    large = max_exact + (np.log(np.maximum(n, 1) / max_exact)
                         / np.log(T5_MAX_DIST / max_exact) * (half - max_exact)).astype(np.int32)
    large = np.minimum(large, half - 1)
    return (ret + np.where(n < max_exact, n, large)).astype(np.int32)


def _dilated_bias(t5_bias):
    rel = np.arange(-HALF_KEYS, HALF_KEYS + 1)
    tab = t5_bias.astype(_F32).T * LOG2E
    by_rel = jnp.stack([
        jnp.take(tab[g * HG_B:(g + 1) * HG_B], jnp.asarray(_t5_buckets(rel * dil)), axis=1)
        for g, dil in enumerate(DILATIONS_B)])
    wide = _toeplitz(by_rel, QBLK_B, KBLK_B + 2 * HALF_KEYS, -HALF_KEYS)
    tiles = jnp.stack([
        wide[..., (2 - variant) * HALF_KEYS:(2 - variant) * HALF_KEYS + KBLK_B]
        for variant in range(3)])
    tiles = tiles.reshape(3, N_GROUPS_B, HG_B // 2, 2 * QBLK_B, KBLK_B)
    return jnp.transpose(tiles, (2, 1, 0, 3, 4))


def _dilated_call(b0, b1, b2, bias):
    B = b0.shape[0]
    T = b0.shape[2]
    chunk = min(CHUNK_B, T)
    assert T % chunk == 0 and chunk % (QBLK_B * max(DILATIONS_B)) == 0
    assert T // max(DILATIONS_B) >= KBLK_B and DILATIONS_B[0] == 1
    in_specs, args = [], []
    for arr, dil in zip((b0, b1, b2), DILATIONS_B):
        sub_len = T // dil
        in_specs += [
            pl.BlockSpec((1, dil, chunk // dil, LANES), lambda b, p, j: (b, 0, j, p)),
            pl.BlockSpec((1, dil, sub_len, LANES), lambda b, p, j: (b, 0, 0, 2 + p)),
            pl.BlockSpec((1, dil, sub_len, LANES), lambda b, p, j: (b, 0, 0, 4 + p)),
        ]
        args += [arr, arr, arr]
    in_specs.append(pl.BlockSpec((1, N_GROUPS_B, 3, 2 * QBLK_B, KBLK_B),
                                 lambda b, p, j: (p, 0, 0, 0, 0)))
    args.append(bias)
    return pl.pallas_call(
        _dilated_kernel,
        grid=(B, HG_B // 2, T // chunk),
        in_specs=in_specs,
        out_specs=pl.BlockSpec((1, chunk, LANES), lambda b, p, j: (b, j, p)),
        out_shape=jax.ShapeDtypeStruct((B, T, GB_W), _BF16),
        scratch_shapes=[pltpu.VMEM((N_GROUPS_B, chunk, LANES), _F32)] * 3
        + [pltpu.VMEM((3, DEINT_STEP, chunk // DEINT_STEP, LANES), _F32)],
        compiler_params=pltpu.CompilerParams(
            dimension_semantics=("arbitrary", "arbitrary", "arbitrary"),
            vmem_limit_bytes=VMEM_LIMIT),
        name="dilated",
    )(*args)


def _mlp_kernel(x_ref, oa_ref, ob_ref, gate_ref, wa_ref, wb_ref, wo_ref, gmlp_ref, wup_ref,
                wdn_ref, y_ref):
    ya = jnp.dot(oa_ref[...], wa_ref[...], preferred_element_type=_F32)
    yb = jnp.dot(ob_ref[...], wb_ref[...], preferred_element_type=_F32)
    ga = gate_ref[:, :D_MODEL].astype(_F32)
    gb = gate_ref[:, D_MODEL:].astype(_F32)
    merged = (ga * ya + gb * yb).astype(_BF16)
    x1 = x_ref[...] + jnp.dot(merged, wo_ref[...], preferred_element_type=_F32)
    ms = jnp.mean(x1 * x1, axis=-1, keepdims=True)
    hm = (x1 * lax.rsqrt(ms + EPS) * gmlp_ref[...]).astype(_BF16)
    acc = x1
    for c in range(D_FF // FF_CHUNK):
        sl = slice(c * FF_CHUNK, (c + 1) * FF_CHUNK)
        u = jnp.maximum(jnp.dot(hm, wup_ref[:, sl], preferred_element_type=_F32), 0.0)
        acc = acc + jnp.dot((u * u).astype(_BF16), wdn_ref[sl, :], preferred_element_type=_F32)
    y_ref[...] = acc


def _mlp_call(x2, oa2, ob2, gates2, wa, wb, wo, gmlp, wup, wdn):
    n_tok, D = x2.shape
    tm = TM_MLP
    assert n_tok % tm == 0
    tok = lambda w: pl.BlockSpec((tm, w), lambda i: (i, 0))
    return pl.pallas_call(
        _mlp_kernel,
        grid=(n_tok // tm,),
        in_specs=[tok(D), tok(QA_W), tok(GB_W), tok(2 * D_MODEL),
                  _resident((QA_W, D)), _resident((GB_W, D)), _resident((D, D)),
                  _resident((1, D)), _resident((D, D_FF)), _resident((D_FF, D))],
        out_specs=tok(D),
        out_shape=jax.ShapeDtypeStruct((n_tok, D), _F32),
        compiler_params=pltpu.CompilerParams(
            dimension_semantics=("arbitrary",), vmem_limit_bytes=VMEM_LIMIT_DENSE),
        name="mlp",
    )(x2, oa2, ob2, gates2, wa, wb, wo, gmlp, wup, wdn)


def _encoder_layer(x, p):
    B, T, D = x.shape
    qa, ka, va, b0, b1, b2, gates = _proj_call(
        x, p["gmix"], p["w_in"], p["gqa"], p["gka"], p["gqb"], p["gkb"])
    oa = _natten_call(qa, ka, va, p["bias_a"])
    ob = _dilated_call(b0, b1, b2, p["bias_b"])
    y = _mlp_call(x.reshape(B * T, D), oa.reshape(B * T, QA_W), ob.reshape(B * T, GB_W),
                  gates.reshape(B * T, 2 * D_MODEL), p["wa"], p["wb"], p["wo"], p["gmlp"],
                  p["wup"], p["wdn"])
    return y.reshape(B, T, D)


def _layer_params(norm_mix, w_in, q_norm_a, k_norm_a, q_norm_b, k_norm_b, rpb_a, t5_bias,
                  w_branch_a, w_branch_b, w_out, norm_mlp, w_up, w_down):
    tile_gain = lambda g: jnp.tile(g.astype(_F32), NORM_CHUNK // HEAD_DIM).reshape(1, NORM_CHUNK)
    return dict(
        gmix=norm_mix.astype(_F32).reshape(1, D_MODEL), w_in=w_in.astype(_BF16),
        gqa=tile_gain(q_norm_a), gka=tile_gain(k_norm_a),
        gqb=tile_gain(q_norm_b), gkb=tile_gain(k_norm_b),
        bias_a=_natten_bias(rpb_a).astype(_BF16), bias_b=_dilated_bias(t5_bias),
        wa=w_branch_a.astype(_BF16), wb=w_branch_b.astype(_BF16), wo=w_out.astype(_BF16),
        gmlp=norm_mlp.astype(_F32).reshape(1, D_MODEL),
        wup=w_up.astype(_BF16), wdn=w_down.astype(_BF16))


def kernel(x_prompt, x_sample, norm_mix, w_in, q_norm_a, k_norm_a, q_norm_b, k_norm_b, rpb_a,
           t5_bias, w_branch_a, w_branch_b, w_out, norm_mlp, w_up, w_down):
    y_prompt, y_sample = x_prompt, x_sample
    for l in range(norm_mix.shape[0]):
        p = _layer_params(norm_mix[l], w_in[l], q_norm_a[l], k_norm_a[l], q_norm_b[l],
                          k_norm_b[l], rpb_a[l], t5_bias, w_branch_a[l], w_branch_b[l],
                          w_out[l], norm_mlp[l], w_up[l], w_down[l])
        y_prompt = _encoder_layer(y_prompt, p)
        y_sample = _encoder_layer(y_sample, p)
    return (y_prompt, y_sample)
```
